```python
import math
import jax, jax.numpy as jnp
from jax import lax
import numpy as np

D_MODEL = 1024
BATCH = 8
SEQ = 2048
DEPTH = 4
DEC_BATCH = 128
DEC_SEQ = 8
PAST_LEN = 2048
PAGE_SIZE = 128

HEAD_DIM = 64
DSA_HEADS = 8
DSA_KV_HEADS = 4
IDX_HEADS = 8
IDX_DIM = 64
TOPK_MAX = 256
CONV_WIDTH = 512
CONV_K = 3
FOX_HEADS = 8
FOX_KV_HEADS = 4
D_FF = 2816
ROPE_THETA = 500000.0
ROPE_DIM = HEAD_DIM // 4
Q_BLOCK = 128
NORM_EPS = 1e-6
FORGET_BIAS_INIT = 2.0
N_BRANCHES = 3

COL_SIZES = (
    DSA_HEADS * HEAD_DIM,
    DSA_KV_HEADS * HEAD_DIM,
    DSA_KV_HEADS * HEAD_DIM,
    IDX_HEADS * IDX_DIM,
    IDX_DIM,
    IDX_HEADS,
    CONV_WIDTH,
    CONV_WIDTH,
    CONV_WIDTH,
    FOX_HEADS * HEAD_DIM,
    FOX_KV_HEADS * HEAD_DIM,
    FOX_KV_HEADS * HEAD_DIM,
    FOX_HEADS,
    N_BRANCHES * D_MODEL,
)
IN_COLS = sum(COL_SIZES)

kernel_name = 'hybrid_dsa_conv_fox_macaron_step'


def rmsnorm(x, g):
    xf = x.astype(jnp.float32)
    y = xf * lax.rsqrt(jnp.mean(xf * xf, axis=-1, keepdims=True) + NORM_EPS)
    return (y * g.astype(jnp.float32)).astype(x.dtype)


def swiglu(x, w_gu, w_dn):
    a, b = jnp.split(x @ w_gu, 2, axis=-1)
    return (jax.nn.silu(a) * b) @ w_dn


def partial_rope(x, pos):
    half = ROPE_DIM // 2
    inv = jnp.power(jnp.float32(ROPE_THETA), -jnp.arange(half, dtype=jnp.float32) * (2.0 / ROPE_DIM))
    ang = pos.astype(jnp.float32)[:, None] * inv[None, :]
    cos = jnp.cos(ang)[:, None, :]
    sin = jnp.sin(ang)[:, None, :]
    xr = x[..., :ROPE_DIM].astype(jnp.float32)
    x1, x2 = xr[..., :half], xr[..., half:]
    rot = jnp.concatenate([x1 * cos - x2 * sin, x1 * sin + x2 * cos], axis=-1).astype(x.dtype)
    return jnp.concatenate([rot, x[..., ROPE_DIM:]], axis=-1)


def to_blocks(a, nb):
    return a.reshape(a.shape[0], nb, a.shape[1] // nb, *a.shape[2:]).swapaxes(0, 1)


def from_blocks(o):
    o = o.swapaxes(0, 1)
    return o.reshape(o.shape[0], o.shape[1] * o.shape[2], o.shape[3])


def dsa_attention(q, q_idx, w_idx, k_all, v_all, k_idx_all, q_offset):
    B, T = q.shape[0], q.shape[1]
    L = k_all.shape[1]
    top_k = min(TOPK_MAX, L // 4)
    blk = math.gcd(T, Q_BLOCK)
    nb = T // blk
    G = DSA_HEADS // DSA_KV_HEADS
    k_pos = jnp.arange(L, dtype=jnp.int32)
    q_pos = q_offset + jnp.arange(T, dtype=jnp.int32)

    def block(args):
        qb, qib, wib, pb = args
        s = jnp.einsum('bthd,bsd->bths', qib, k_idx_all).astype(jnp.float32) * (IDX_DIM ** -0.5)
        score = jnp.einsum('bths,bth->bts', jax.nn.relu(s), wib.astype(jnp.float32))
        visible = k_pos[None, None, :] <= pb[None, :, None]
        score = jnp.where(visible, score, -jnp.inf)
        vals, idx = lax.top_k(score, top_k)
        valid = vals > -jnp.inf
        k_sel = jax.vmap(lambda kk, ii: kk[ii])(k_all, idx)
        v_sel = jax.vmap(lambda vv, ii: vv[ii])(v_all, idx)
        qg = qb.reshape(B, blk, DSA_KV_HEADS, G, HEAD_DIM)
        logits = jnp.einsum('btgrd,btkgd->btgrk', qg, k_sel).astype(jnp.float32) * (HEAD_DIM ** -0.5)
        logits = jnp.where(valid[:, :, None, None, :], logits, -jnp.inf)
        p = jax.nn.softmax(logits, axis=-1).astype(v_sel.dtype)
        o = jnp.einsum('btgrk,btkgd->btgrd', p, v_sel)
        return o.reshape(B, blk, DSA_HEADS * HEAD_DIM)

    out = lax.map(block, (to_blocks(q, nb), to_blocks(q_idx, nb), to_blocks(w_idx, nb), q_pos.reshape(nb, blk)))
    return from_blocks(out)


def fox_attention(q, k_all, v_all, logf_all, q_offset):
    B, T = q.shape[0], q.shape[1]
    L = k_all.shape[1]
    blk = math.gcd(T, Q_BLOCK)
    nb = T // blk
    G = FOX_HEADS // FOX_KV_HEADS
    F = jnp.cumsum(logf_all.astype(jnp.float32), axis=1)
    F_q = F[:, q_offset:q_offset + T]
    F_k = F.reshape(B, L, FOX_KV_HEADS, G).transpose(0, 2, 3, 1)
    k_pos = jnp.arange(L, dtype=jnp.int32)
    q_pos = q_offset + jnp.arange(T, dtype=jnp.int32)

    def block(args):
        qb, fqb, pb = args
        qg = qb.reshape(B, blk, FOX_KV_HEADS, G, HEAD_DIM)
        logits = jnp.einsum('btgrd,bsgd->bgrts', qg, k_all).astype(jnp.float32) * (HEAD_DIM ** -0.5)
        fq = fqb.reshape(B, blk, FOX_KV_HEADS, G).transpose(0, 2, 3, 1)[..., None]
        logits = logits + fq - F_k[:, :, :, None, :]
        causal = k_pos[None, :] <= pb[:, None]
        logits = jnp.where(causal, logits, -jnp.inf)
        p = jax.nn.softmax(logits, axis=-1).astype(v_all.dtype)
        o = jnp.einsum('bgrts,bsgd->btgrd', p, v_all)
        return o.reshape(B, blk, FOX_HEADS * HEAD_DIM)

    out = lax.map(block, (to_blocks(q, nb), to_blocks(F_q, nb), q_pos.reshape(nb, blk)))
    return from_blocks(out)


def short_conv(z, prev, w):
    T = z.shape[1]
    zp = jnp.concatenate([prev.astype(z.dtype), z], axis=1)
    out = sum(w[j] * zp[:, j:j + T] for j in range(CONV_K))
    return out, zp[:, zp.shape[1] - (CONV_K - 1):]


def mixer(u, past, q_offset, w_in, b_forget, conv_w, w_br_dsa, w_br_conv, w_br_fox, w_out):
    B, T, _ = u.shape
    split_idx = [int(i) for i in np.cumsum(COL_SIZES)[:-1]]
    (q_dsa, k_dsa, v_dsa, q_idx, k_idx, w_idx, zb, zc, zx,
     q_fox, k_fox, v_fox, f_logit, g) = jnp.split(u @ w_in, split_idx, axis=-1)
    pos = q_offset + jnp.arange(T, dtype=jnp.int32)
    q_dsa = partial_rope(q_dsa.reshape(B, T, DSA_HEADS, HEAD_DIM), pos)
    k_dsa = partial_rope(k_dsa.reshape(B, T, DSA_KV_HEADS, HEAD_DIM), pos)
    v_dsa = v_dsa.reshape(B, T, DSA_KV_HEADS, HEAD_DIM)
    q_idx = partial_rope(q_idx.reshape(B, T, IDX_HEADS, IDX_DIM), pos)
    k_idx = partial_rope(k_idx[:, :, None, :], pos)[:, :, 0, :]
    w_idx = w_idx * (IDX_HEADS ** -0.5)
    q_fox = q_fox.reshape(B, T, FOX_HEADS, HEAD_DIM)
    k_fox = k_fox.reshape(B, T, FOX_KV_HEADS, HEAD_DIM)
    v_fox = v_fox.reshape(B, T, FOX_KV_HEADS, HEAD_DIM)
    logf = jax.nn.log_sigmoid(f_logit.astype(jnp.float32) + b_forget.astype(jnp.float32))
    z = zc * zx

    if past is None:
        kd_all, vd_all, ki_all, kf_all, vf_all, lf_all = k_dsa, v_dsa, k_idx, k_fox, v_fox, logf
        prev = jnp.zeros((B, CONV_K - 1, CONV_WIDTH), z.dtype)
    else:
        pk, pv, pki, pfk, pfv, pfl, prev = past
        kd_all = jnp.concatenate([pk, k_dsa], axis=1)
        vd_all = jnp.concatenate([pv, v_dsa], axis=1)
        ki_all = jnp.concatenate([pki, k_idx], axis=1)
        kf_all = jnp.concatenate([pfk, k_fox], axis=1)
        vf_all = jnp.concatenate([pfv, v_fox], axis=1)
        lf_all = jnp.concatenate([pfl.astype(jnp.float32), logf], axis=1)

    o_dsa = dsa_attention(q_dsa, q_idx, w_idx, kd_all, vd_all, ki_all, q_offset)
    conv_out, conv_state = short_conv(z, prev, conv_w)
    o_conv = zb * conv_out
    o_fox = fox_attention(q_fox, kf_all, vf_all, lf_all, q_offset)

    gates = jax.nn.sigmoid(g.astype(jnp.float32)).reshape(B, T, N_BRANCHES, D_MODEL).astype(u.dtype)
    merged = (gates[:, :, 0] * (o_dsa @ w_br_dsa)
              + gates[:, :, 1] * (o_conv @ w_br_conv)
              + gates[:, :, 2] * (o_fox @ w_br_fox))
    return merged @ w_out, (k_dsa, v_dsa, k_idx, k_fox, v_fox, logf, conv_state)


def layer(x, past, q_offset, g_ffn1_pre, g_ffn1_post, w_ffn1_gu, w_ffn1_dn, g_mix_pre, g_mix_post,
          w_in, b_forget, conv_w, w_br_dsa, w_br_conv, w_br_fox, w_out,
          g_ffn2_pre, g_ffn2_post, w_ffn2_gu, w_ffn2_dn):
    h = x + 0.5 * rmsnorm(swiglu(rmsnorm(x, g_ffn1_pre), w_ffn1_gu, w_ffn1_dn), g_ffn1_post)
    m, rows = mixer(rmsnorm(h, g_mix_pre), past, q_offset, w_in, b_forget, conv_w,
                    w_br_dsa, w_br_conv, w_br_fox, w_out)
    h = h + rmsnorm(m, g_mix_post)
    h = h + 0.5 * rmsnorm(swiglu(rmsnorm(h, g_ffn2_pre), w_ffn2_gu, w_ffn2_dn), g_ffn2_post)
    return h, rows


def paged_rows(cache, l, page_table):
    r = cache[l, page_table]
    return r.reshape(r.shape[0], r.shape[1] * r.shape[2], *r.shape[3:])


def setup_inputs(seed: int = 0) -> dict:
    key = jax.random.key(seed)
    ks = jax.random.split(key, 32)
    f32 = jnp.float32
    n_pages = PAST_LEN // PAGE_SIZE
    n_used = DEC_BATCH * n_pages
    n_pool = n_used + max(1, n_used // 4)

    def nrm(k, shape, scale=1.0):
        return scale * jax.random.normal(k, shape, f32)

    def gain(k):
        return 1.0 + nrm(k, (DEPTH, D_MODEL), 0.05)

    page_table = jax.random.permutation(ks[0], n_pool)[:n_used].reshape(DEC_BATCH, n_pages).astype(jnp.int32)
    return {
        'x_prompt': nrm(ks[1], (BATCH, SEQ, D_MODEL)),
        'x_sample': nrm(ks[2], (DEC_BATCH, DEC_SEQ, D_MODEL)),
        'cache_dsa_k': nrm(ks[3], (DEPTH, n_pool, PAGE_SIZE, DSA_KV_HEADS, HEAD_DIM)),
        'cache_dsa_v': nrm(ks[4], (DEPTH, n_pool, PAGE_SIZE, DSA_KV_HEADS, HEAD_DIM)),
        'cache_idx_k': nrm(ks[5], (DEPTH, n_pool, PAGE_SIZE, IDX_DIM)),
        'cache_fox_k': nrm(ks[6], (DEPTH, n_pool, PAGE_SIZE, FOX_KV_HEADS, HEAD_DIM)),
        'cache_fox_v': nrm(ks[7], (DEPTH, n_pool, PAGE_SIZE, FOX_KV_HEADS, HEAD_DIM)),
        'cache_fox_logf': jax.nn.log_sigmoid(FORGET_BIAS_INIT + nrm(ks[8], (DEPTH, n_pool, PAGE_SIZE, FOX_HEADS))),
        'state_conv': nrm(ks[9], (DEPTH, DEC_BATCH, CONV_K - 1, CONV_WIDTH)),
        'page_table': page_table,
        'g_ffn1_pre': gain(ks[10]),
        'g_ffn1_post': gain(ks[11]),
        'w_ffn1_gu': nrm(ks[12], (DEPTH, D_MODEL, 2 * D_FF), D_MODEL ** -0.5),
        'w_ffn1_dn': nrm(ks[13], (DEPTH, D_FF, D_MODEL), D_FF ** -0.5),
        'g_mix_pre': gain(ks[14]),
        'g_mix_post': gain(ks[15]),
        'w_in': nrm(ks[16], (DEPTH, D_MODEL, IN_COLS), D_MODEL ** -0.5),
        'b_forget': FORGET_BIAS_INIT + nrm(ks[17], (DEPTH, FOX_HEADS), 0.5),
        'conv_w': nrm(ks[18], (DEPTH, CONV_K, CONV_WIDTH), CONV_K ** -0.5),
        'w_br_dsa': nrm(ks[19], (DEPTH, DSA_HEADS * HEAD_DIM, D_MODEL), (DSA_HEADS * HEAD_DIM) ** -0.5),
        'w_br_conv': nrm(ks[20], (DEPTH, CONV_WIDTH, D_MODEL), CONV_WIDTH ** -0.5),
        'w_br_fox': nrm(ks[21], (DEPTH, FOX_HEADS * HEAD_DIM, D_MODEL), (FOX_HEADS * HEAD_DIM) ** -0.5),
        'w_out': nrm(ks[22], (DEPTH, D_MODEL, D_MODEL), D_MODEL ** -0.5),
        'g_ffn2_pre': gain(ks[23]),
        'g_ffn2_post': gain(ks[24]),
        'w_ffn2_gu': nrm(ks[25], (DEPTH, D_MODEL, 2 * D_FF), D_MODEL ** -0.5),
        'w_ffn2_dn': nrm(ks[26], (DEPTH, D_FF, D_MODEL), D_FF ** -0.5),
    }


def reference(x_prompt, x_sample, cache_dsa_k, cache_dsa_v, cache_idx_k, cache_fox_k, cache_fox_v,
              cache_fox_logf, state_conv, page_table,
              g_ffn1_pre, g_ffn1_post, w_ffn1_gu, w_ffn1_dn, g_mix_pre, g_mix_post,
              w_in, b_forget, conv_w, w_br_dsa, w_br_conv, w_br_fox, w_out,
              g_ffn2_pre, g_ffn2_post, w_ffn2_gu, w_ffn2_dn):
    past_len = page_table.shape[1] * PAGE_SIZE
    xp, xs = x_prompt, x_sample
    rows_p = [[] for _ in range(7)]
    rows_s = [[] for _ in range(7)]
    for l in range(DEPTH):
        lw = (g_ffn1_pre[l], g_ffn1_post[l], w_ffn1_gu[l], w_ffn1_dn[l], g_mix_pre[l], g_mix_post[l],
              w_in[l], b_forget[l], conv_w[l], w_br_dsa[l], w_br_conv[l], w_br_fox[l], w_out[l],
              g_ffn2_pre[l], g_ffn2_post[l], w_ffn2_gu[l], w_ffn2_dn[l])
        past = (paged_rows(cache_dsa_k, l, page_table), paged_rows(cache_dsa_v, l, page_table),
                paged_rows(cache_idx_k, l, page_table), paged_rows(cache_fox_k, l, page_table),
                paged_rows(cache_fox_v, l, page_table), paged_rows(cache_fox_logf, l, page_table),
                state_conv[l])
        xp, new_p = layer(xp, None, 0, *lw)
        xs, new_s = layer(xs, past, past_len, *lw)
        for i in range(7):
            rows_p[i].append(new_p[i])
            rows_s[i].append(new_s[i])
    sp = [jnp.stack(r, axis=0) for r in rows_p]
    ss = [jnp.stack(r, axis=0) for r in rows_s]
    return (xp, xs, sp[0], sp[1], sp[2], sp[3], sp[4], sp[5], sp[6],
            ss[0], ss[1], ss[2], ss[3], ss[4], ss[5], ss[6])
```

```python
import functools
import math

import jax
import jax.numpy as jnp
from jax import lax
from jax.experimental import pallas as pl
from jax.experimental.pallas import tpu as pltpu

HEAD_DIM = 64
DSA_HEADS = 8
DSA_KV_HEADS = 4
IDX_HEADS = 8
IDX_DIM = 64
TOPK_MAX = 256
CONV_WIDTH = 512
CONV_K = 3
FOX_HEADS = 8
FOX_KV_HEADS = 4
ROPE_THETA = 500000.0
ROPE_DIM = HEAD_DIM // 4
NORM_EPS = 1e-6
N_BRANCHES = 3
PAGE_SIZE = 128

LANES = 128
KV_WIDTH = DSA_KV_HEADS * HEAD_DIM
Q_WIDTH = DSA_HEADS * HEAD_DIM
VMEM_LIMIT = 56 * 1024 * 1024

F32 = jnp.float32
BF16 = jnp.bfloat16
I32 = jnp.int32
NEG = -1e30
INT_MIN = -(2 ** 31)

assert DSA_HEADS == FOX_HEADS and DSA_KV_HEADS == FOX_KV_HEADS
assert DSA_HEADS // DSA_KV_HEADS == 2 and 2 * HEAD_DIM == LANES


def _rms(x, g):
    return x * lax.rsqrt(jnp.mean(x * x, axis=-1, keepdims=True) + NORM_EPS) * g


def _nt_dot(a, b):
    return lax.dot_general(a, b, (((1,), (1,)), ((), ())), preferred_element_type=F32)


def _params(*sem):
    return pltpu.CompilerParams(dimension_semantics=sem, vmem_limit_bytes=VMEM_LIMIT)


def _ffn_kernel(x_ref, gpre_ref, gpost_ref, wa_ref, wb_ref, wd_ref, o_ref, xn_ref, acc_ref, *, nf):
    j = pl.program_id(1)

    @pl.when(j == 0)
    def _():
        xn_ref[...] = _rms(x_ref[...], gpre_ref[...]).astype(BF16)
        acc_ref[...] = jnp.zeros_like(acc_ref)

    xn = xn_ref[...]
    a = jnp.dot(xn, wa_ref[...], preferred_element_type=F32)
    b = jnp.dot(xn, wb_ref[...], preferred_element_type=F32)
    act = (a * jax.nn.sigmoid(a) * b).astype(BF16)
    acc_ref[...] += jnp.dot(act, wd_ref[...], preferred_element_type=F32)

    @pl.when(j == nf - 1)
    def _():
        o_ref[...] = x_ref[...] + 0.5 * _rms(acc_ref[...], gpost_ref[...])


def _ffn(x, g_pre, g_post, w_gu, w_dn, l, *, tm, tf):
    n, d = x.shape
    dff = w_dn.shape[1]
    nf = dff // tf
    return pl.pallas_call(
        functools.partial(_ffn_kernel, nf=nf),
        grid=(n // tm, nf),
        in_specs=[
            pl.BlockSpec((tm, d), lambda i, j: (i, 0)),
            pl.BlockSpec((None, 1, d), lambda i, j: (l, 0, 0)),
            pl.BlockSpec((None, 1, d), lambda i, j: (l, 0, 0)),
            pl.BlockSpec((None, d, tf), lambda i, j: (l, 0, j)),
            pl.BlockSpec((None, d, tf), lambda i, j: (l, 0, j + nf)),
            pl.BlockSpec((None, tf, d), lambda i, j: (l, j, 0)),
        ],
        out_specs=pl.BlockSpec((tm, d), lambda i, j: (i, 0)),
        out_shape=jax.ShapeDtypeStruct((n, d), F32),
        scratch_shapes=[pltpu.VMEM((tm, d), BF16), pltpu.VMEM((tm, d), F32)],
        compiler_params=_params("parallel", "arbitrary"),
        name="ffn",
    )(x, g_pre, g_post, w_gu, w_gu, w_dn)


ROPE_COLS = Q_WIDTH + KV_WIDTH + Q_WIDTH + LANES
PLAIN_COLS = KV_WIDTH + CONV_WIDTH + Q_WIDTH + 2 * KV_WIDTH


def _log_sigmoid(x):
    return jnp.minimum(x, 0.0) - jnp.log1p(jnp.exp(-jnp.abs(x)))


def _inproj_kernel(x_ref, g_ref, cos_ref, sa_ref, sb_ref, bf_ref, wr_ref, wp_ref, wz_ref, wwi_ref, wfl_ref,
                   qd_o, kd_o, qi_o, ki_o, vd_o, zb_o, qf_o, kf_o, vf_o, z_o, wi_o, lf_o):
    xn = _rms(x_ref[...], g_ref[...]).astype(BF16)
    cos, sa, sb = cos_ref[...], sa_ref[...], sb_ref[...]

    r = jnp.dot(xn, wr_ref[...], preferred_element_type=F32)
    outs = []
    for c in range(ROPE_COLS // LANES):
        v = r[:, c * LANES:(c + 1) * LANES]
        up = pltpu.roll(v, LANES - ROPE_DIM // 2, axis=1)
        dn = pltpu.roll(v, ROPE_DIM // 2, axis=1)
        outs.append(v * cos + up * sa + dn * sb)
    nq = Q_WIDTH // LANES
    nk = KV_WIDTH // LANES
    qd_o[...] = jnp.concatenate(outs[:nq], axis=1)
    kd_o[...] = jnp.concatenate(outs[nq:nq + nk], axis=1)
    qi_o[...] = jnp.concatenate(outs[nq + nk:2 * nq + nk], axis=1)
    ki_o[...] = outs[2 * nq + nk][:, :IDX_DIM]

    p = jnp.dot(xn, wp_ref[...], preferred_element_type=F32)
    o = 0
    for ref, w in ((vd_o, KV_WIDTH), (zb_o, CONV_WIDTH), (qf_o, Q_WIDTH), (kf_o, KV_WIDTH), (vf_o, KV_WIDTH)):
        ref[...] = p[:, o:o + w]
        o += w

    zz = jnp.dot(xn, wz_ref[...], preferred_element_type=F32)
    z_o[...] = zz[:, :CONV_WIDTH] * zz[:, CONV_WIDTH:]

    wi = jnp.dot(xn, wwi_ref[...], preferred_element_type=F32)
    wi_o[...] = wi * (IDX_HEADS ** -0.5)
    fl = jnp.dot(xn, wfl_ref[...], preferred_element_type=F32)
    lf_o[...] = _log_sigmoid(fl + bf_ref[...])


def _inproj(x, g, cos, sa, sb, b_forget, wr, wp, wz, wwi, wfl, l, *, tm):
    n, d = x.shape
    tok = lambda w: pl.BlockSpec((tm, w), lambda i: (i, 0))
    lay = lambda a, b: pl.BlockSpec((None, a, b), lambda i: (l, 0, 0))
    widths = (Q_WIDTH, KV_WIDTH, Q_WIDTH, IDX_DIM, KV_WIDTH, CONV_WIDTH, Q_WIDTH, KV_WIDTH, KV_WIDTH,
              CONV_WIDTH, IDX_HEADS, FOX_HEADS)
    return pl.pallas_call(
        _inproj_kernel,
        grid=(n // tm,),
        in_specs=[tok(d), lay(1, d), tok(LANES), tok(LANES), tok(LANES), lay(1, FOX_HEADS),
                  lay(d, ROPE_COLS), lay(d, PLAIN_COLS), lay(d, 2 * CONV_WIDTH),
                  lay(d, IDX_HEADS), lay(d, FOX_HEADS)],
        out_specs=[tok(w) for w in widths],
        out_shape=[jax.ShapeDtypeStruct((n, w), F32) for w in widths],
        compiler_params=_params("parallel"),
        name="inproj",
    )(x, g, cos, sa, sb, b_forget, wr, wp, wz, wwi, wfl)


def _lane_cumsum(x):
    rows, width = x.shape
    lane = lax.broadcasted_iota(I32, (rows, LANES), 1)
    carry = jnp.zeros((rows, 1), F32)
    out = []
    for c in range(width // LANES):
        v = x[:, c * LANES:(c + 1) * LANES]
        s = 1
        while s < LANES:
            v = v + jnp.where(lane >= s, pltpu.roll(v, s, axis=1), 0.0)
            s *= 2
        v = v + carry
        carry = jnp.sum(jnp.where(lane == LANES - 1, v, 0.0), axis=1, keepdims=True)
        out.append(v)
    return jnp.concatenate(out, axis=1)


def _cumsum_kernel(x_ref, o_ref):
    o_ref[...] = _lane_cumsum(x_ref[...])


def _cumsum_prompt(lf_t):
    b, h, t = lf_t.shape
    return pl.pallas_call(
        _cumsum_kernel,
        grid=(b,),
        in_specs=[pl.BlockSpec((None, h, t), lambda i: (i, 0, 0))],
        out_specs=pl.BlockSpec((None, h, t), lambda i: (i, 0, 0)),
        out_shape=jax.ShapeDtypeStruct((b, h, t), F32),
        compiler_params=_params("parallel"),
        name="cumsum_prompt",
    )(lf_t)


def _cumsum_sample_kernel(pt_ref, *refs, n_pages):
    pages, new_ref, o_ref = refs[:n_pages], refs[n_pages], refs[n_pages + 1]
    x = jnp.concatenate([p[...] for p in pages] + [new_ref[...]], axis=1)
    o_ref[...] = _lane_cumsum(x)


def _page_specs(n_pages, rows, cols, l):
    return [pl.BlockSpec((None, None, rows, cols),
                         functools.partial(lambda b, pt, p: (l, pt[b * n_pages + p], 0, 0), p=p))
            for p in range(n_pages)]


def _cumsum_sample(page_table, cache_lf_t, new_lf_t, l):
    r, n_pages = page_table.shape
    h = new_lf_t.shape[1]
    width = n_pages * PAGE_SIZE + LANES
    grid_spec = pltpu.PrefetchScalarGridSpec(
        num_scalar_prefetch=1,
        grid=(r,),
        in_specs=_page_specs(n_pages, h, PAGE_SIZE, l) + [pl.BlockSpec((None, h, LANES), lambda b, pt: (b, 0, 0))],
        out_specs=pl.BlockSpec((None, h, width), lambda b, pt: (b, 0, 0)),
    )
    return pl.pallas_call(
        functools.partial(_cumsum_sample_kernel, n_pages=n_pages),
        grid_spec=grid_spec,
        out_shape=jax.ShapeDtypeStruct((r, h, width), F32),
        compiler_params=_params("parallel"),
        name="cumsum_sample",
    )(page_table.reshape(-1), *([cache_lf_t] * n_pages), new_lf_t)


def _q_block_diag(q):
    t = q.shape[0]
    lane = lax.broadcasted_iota(I32, (t, LANES), 1)
    zeros = jnp.zeros((t, LANES), F32)
    blocks = []
    for h in range(DSA_HEADS):
        g, r = divmod(h, 2)
        src = q[:, g * LANES:(g + 1) * LANES]
        if r != g % 2:
            src = pltpu.roll(src, HEAD_DIM, axis=1)
        keep = (lane < HEAD_DIM) if g % 2 == 0 else (lane >= HEAD_DIM)
        m = jnp.where(keep, src, 0.0)
        blocks.append(jnp.concatenate([m, zeros] if g // 2 == 0 else [zeros, m], axis=1))
    return jnp.concatenate(blocks, axis=0).astype(BF16)


def _head_rows(q):
    return jnp.concatenate([q[:, h * IDX_DIM:(h + 1) * IDX_DIM] for h in range(IDX_HEADS)], axis=0)


def _extract_heads(acc, t):
    outs = []
    for h in range(DSA_HEADS):
        g = h // 2
        outs.append(acc[h * t:(h + 1) * t, g * HEAD_DIM:(g + 1) * HEAD_DIM])
    return jnp.concatenate(outs, axis=1)


def _sort_key(score):
    bits = pltpu.bitcast(score, I32)
    return jnp.where(bits < 0, bits ^ 0x7FFFFFFF, bits)


def _count(mask):
    return jnp.sum(jnp.where(mask, 1.0, 0.0), axis=1, keepdims=True)


def _topk_select(key, top_k, n_bits_idx):
    t, length = key.shape
    kf = float(top_k)

    def thr_step(i, cand):
        trial = cand ^ lax.shift_left(jnp.int32(1), 31 - i)
        return jnp.where(_count(key >= trial) >= kf, trial, cand)

    thr = lax.fori_loop(0, 32, thr_step, jnp.full((t, 1), INT_MIN, I32))
    eq = key == thr
    need = kf - _count(key > thr)
    idx = lax.broadcasted_iota(I32, (t, length), 1)

    def idx_step(i, lo):
        trial = lo | lax.shift_left(jnp.int32(1), n_bits_idx - 1 - i)
        return jnp.where(_count(eq & (idx < trial)) < need, trial, lo)

    j = lax.fori_loop(0, n_bits_idx, idx_step, jnp.zeros((t, 1), I32))
    return thr, j


def _neg_inf_key():
    return INT_MIN + 0x7FFFFF


def _fox_prompt_kernel(q_ref, k_ref, v_ref, fq_ref, fk_ref, o_ref, m_ref, l_ref, acc_ref, *, tq, tk):
    i = pl.program_id(1)
    qbd = _q_block_diag(q_ref[...] * (HEAD_DIM ** -0.5))
    fq = fq_ref[...]
    m_ref[...] = jnp.full_like(m_ref, NEG)
    l_ref[...] = jnp.zeros_like(l_ref)
    acc_ref[...] = jnp.zeros_like(acc_ref)
    qpos = i * tq + lax.broadcasted_iota(I32, (tq, tk), 0)
    kiota = lax.broadcasted_iota(I32, (tq, tk), 1)

    def chunk(c, carry):
        start = pl.multiple_of(c * tk, tk)
        k = k_ref[pl.ds(start, tk), :].astype(BF16)
        v = v_ref[pl.ds(start, tk), :].astype(BF16)
        s = _nt_dot(qbd, k)
        fk = fk_ref[c]
        causal = (kiota + c * tk) <= qpos
        for h in range(FOX_HEADS):
            rows = slice(h * tq, (h + 1) * tq)
            sh = s[rows] + (fq[:, h:h + 1] - fk[h:h + 1, :])
            sh = jnp.where(causal, sh, NEG)
            m_old = m_ref[rows]
            m_new = jnp.maximum(m_old, jnp.max(sh, axis=1, keepdims=True))
            p = jnp.exp(sh - m_new)
            alpha = jnp.exp(m_old - m_new)
            l_ref[rows] = alpha * l_ref[rows] + jnp.sum(p, axis=1, keepdims=True)
            acc_ref[rows] = alpha * acc_ref[rows] + jnp.dot(p.astype(BF16), v, preferred_element_type=F32)
            m_ref[rows] = m_new
        return carry

    lax.fori_loop(0, (i * tq + tq - 1) // tk + 1, chunk, 0)
    o_ref[...] = _extract_heads(acc_ref[...] / l_ref[...], tq)


def _fox_prompt(q, k, v, fq, fk, *, tq, tk):
    b, t, _ = q.shape
    return pl.pallas_call(
        functools.partial(_fox_prompt_kernel, tq=tq, tk=tk),
        grid=(b, t // tq),
        in_specs=[
            pl.BlockSpec((None, tq, Q_WIDTH), lambda bi, i: (bi, i, 0)),
            pl.BlockSpec((None, t, KV_WIDTH), lambda bi, i: (bi, 0, 0)),
            pl.BlockSpec((None, t, KV_WIDTH), lambda bi, i: (bi, 0, 0)),
            pl.BlockSpec((None, tq, FOX_HEADS), lambda bi, i: (bi, i, 0)),
            pl.BlockSpec((None, t // tk, FOX_HEADS, tk), lambda bi, i: (bi, 0, 0, 0)),
        ],
        out_specs=pl.BlockSpec((None, tq, Q_WIDTH), lambda bi, i: (bi, i, 0)),
        out_shape=jax.ShapeDtypeStruct((b, t, Q_WIDTH), F32),
        scratch_shapes=[pltpu.VMEM((FOX_HEADS * tq, 1), F32), pltpu.VMEM((FOX_HEADS * tq, 1), F32),
                        pltpu.VMEM((FOX_HEADS * tq, KV_WIDTH), F32)],
        compiler_params=_params("parallel", "parallel"),
        name="fox_prompt",
    )(q, k, v, fq, fk)


def _dsa_prompt_kernel(q_ref, qi_ref, wi_ref, k_ref, v_ref, ki_ref, o_ref,
                       key_ref, m_ref, l_ref, acc_ref, *, tq, tk, top_k):
    i = pl.program_id(1)
    t_all = k_ref.shape[0]
    n_vis = (i * tq + tq - 1) // tk + 1
    qpos = i * tq + lax.broadcasted_iota(I32, (tq, tk), 0)
    kiota = lax.broadcasted_iota(I32, (tq, tk), 1)

    key_ref[...] = jnp.full_like(key_ref, _neg_inf_key())
    qi = _head_rows(qi_ref[...]).astype(BF16)
    wi = wi_ref[...] * (IDX_DIM ** -0.5)

    def score_chunk(c, carry):
        start = pl.multiple_of(c * tk, tk)
        s = _nt_dot(qi, ki_ref[pl.ds(start, tk), :].astype(BF16))
        score = jnp.zeros((tq, tk), F32)
        for h in range(IDX_HEADS):
            score = score + jnp.maximum(s[h * tq:(h + 1) * tq], 0.0) * wi[:, h:h + 1]
        score = jnp.where(score == 0.0, 0.0, score)
        score = jnp.where((kiota + c * tk) <= qpos, score, -jnp.inf)
        key_ref[:, pl.ds(start, tk)] = _sort_key(score)
        return carry

    lax.fori_loop(0, n_vis, score_chunk, 0)

    thr, jmax = _topk_select(key_ref[...], top_k, int(math.log2(t_all)))

    qbd = _q_block_diag(q_ref[...] * (HEAD_DIM ** -0.5))
    m_ref[...] = jnp.full_like(m_ref, NEG)
    l_ref[...] = jnp.zeros_like(l_ref)
    acc_ref[...] = jnp.zeros_like(acc_ref)

    def attn_chunk(c, carry):
        start = pl.multiple_of(c * tk, tk)
        k = k_ref[pl.ds(start, tk), :].astype(BF16)
        v = v_ref[pl.ds(start, tk), :].astype(BF16)
        s = _nt_dot(qbd, k)
        key = key_ref[:, pl.ds(start, tk)]
        sel = (key > thr) | ((key == thr) & ((kiota + c * tk) <= jmax))
        bias = jnp.where(sel & (key > _neg_inf_key()), 0.0, NEG)
        for h in range(DSA_HEADS):
            rows = slice(h * tq, (h + 1) * tq)
            sh = s[rows] + bias
            m_old = m_ref[rows]
            m_new = jnp.maximum(m_old, jnp.max(sh, axis=1, keepdims=True))
            p = jnp.exp(sh - m_new)
            alpha = jnp.exp(m_old - m_new)
            l_ref[rows] = alpha * l_ref[rows] + jnp.sum(p, axis=1, keepdims=True)
            acc_ref[rows] = alpha * acc_ref[rows] + jnp.dot(p.astype(BF16), v, preferred_element_type=F32)
            m_ref[rows] = m_new
        return carry

    lax.fori_loop(0, n_vis, attn_chunk, 0)
    o_ref[...] = _extract_heads(acc_ref[...] / l_ref[...], tq)


def _dsa_prompt(q, qi, wi, k, v, ki, *, tq, tk):
    b, t, _ = q.shape
    top_k = min(TOPK_MAX, t // 4)
    return pl.pallas_call(
        functools.partial(_dsa_prompt_kernel, tq=tq, tk=tk, top_k=top_k),
        grid=(b, t // tq),
        in_specs=[
            pl.BlockSpec((None, tq, Q_WIDTH), lambda bi, i: (bi, i, 0)),
            pl.BlockSpec((None, tq, Q_WIDTH), lambda bi, i: (bi, i, 0)),
            pl.BlockSpec((None, tq, IDX_HEADS), lambda bi, i: (bi, i, 0)),
            pl.BlockSpec((None, t, KV_WIDTH), lambda bi, i: (bi, 0, 0)),
            pl.BlockSpec((None, t, KV_WIDTH), lambda bi, i: (bi, 0, 0)),
            pl.BlockSpec((None, t, IDX_DIM), lambda bi, i: (bi, 0, 0)),
        ],
        out_specs=pl.BlockSpec((None, tq, Q_WIDTH), lambda bi, i: (bi, i, 0)),
        out_shape=jax.ShapeDtypeStruct((b, t, Q_WIDTH), F32),
        scratch_shapes=[pltpu.VMEM((tq, t), I32),
                        pltpu.VMEM((DSA_HEADS * tq, 1), F32), pltpu.VMEM((DSA_HEADS * tq, 1), F32),
                        pltpu.VMEM((DSA_HEADS * tq, KV_WIDTH), F32)],
        compiler_params=_params("parallel", "parallel"),
        name="dsa_prompt",
    )(q, qi, wi, k, v, ki)


def _pad_rows(x, rows):
    return jnp.concatenate([x, jnp.zeros((rows - x.shape[0], x.shape[1]), x.dtype)], axis=0)


def _softmax_pv(logits, v_blocks, t):
    m = jnp.max(logits, axis=1, keepdims=True)
    p = jnp.exp(logits - m)
    l = jnp.sum(p, axis=1, keepdims=True)
    pb = p.astype(BF16)
    acc = jnp.zeros((logits.shape[0], KV_WIDTH), F32)
    for n, vb in enumerate(v_blocks):
        acc = acc + jnp.dot(pb[:, n * PAGE_SIZE:(n + 1) * PAGE_SIZE], vb, preferred_element_type=F32)
    return _extract_heads(acc / l, t)


def _fox_sample_kernel(pt_ref, *refs, n_pages, t):
    kp, vp = refs[:n_pages], refs[n_pages:2 * n_pages]
    q_ref, kn_ref, vn_ref, fq_ref, fk_ref, o_ref = refs[2 * n_pages:]
    qbd = _q_block_diag(q_ref[...] * (HEAD_DIM ** -0.5))
    k_blocks = [p[...].astype(BF16) for p in kp] + [_pad_rows(kn_ref[...], PAGE_SIZE).astype(BF16)]
    v_blocks = [p[...].astype(BF16) for p in vp] + [_pad_rows(vn_ref[...], PAGE_SIZE).astype(BF16)]
    s = jnp.concatenate([_nt_dot(qbd, kb) for kb in k_blocks], axis=1)
    length = s.shape[1]
    past = n_pages * PAGE_SIZE
    pos = lax.broadcasted_iota(I32, (t, length), 1)
    visible = pos <= past + lax.broadcasted_iota(I32, (t, length), 0)
    fq, fk = fq_ref[...], fk_ref[...]
    rows = []
    for h in range(FOX_HEADS):
        sh = s[h * t:(h + 1) * t] + (fq[:, h:h + 1] - fk[h:h + 1, :])
        rows.append(jnp.where(visible, sh, NEG))
    o_ref[...] = _softmax_pv(jnp.concatenate(rows, axis=0), v_blocks, t)


def _fox_sample(page_table, cache_k, cache_v, q, kn, vn, fq, fk, l):
    r, n_pages = page_table.shape
    t = q.shape[1]
    width = n_pages * PAGE_SIZE + LANES
    req = lambda a, b: pl.BlockSpec((None, a, b), lambda bi, pt: (bi, 0, 0))
    grid_spec = pltpu.PrefetchScalarGridSpec(
        num_scalar_prefetch=1,
        grid=(r,),
        in_specs=(_page_specs(n_pages, PAGE_SIZE, KV_WIDTH, l) * 2
                  + [req(t, Q_WIDTH), req(t, KV_WIDTH), req(t, KV_WIDTH), req(t, FOX_HEADS), req(FOX_HEADS, width)]),
        out_specs=req(t, Q_WIDTH),
    )
    return pl.pallas_call(
        functools.partial(_fox_sample_kernel, n_pages=n_pages, t=t),
        grid_spec=grid_spec,
        out_shape=jax.ShapeDtypeStruct((r, t, Q_WIDTH), F32),
        compiler_params=_params("parallel"),
        name="fox_sample",
    )(page_table.reshape(-1), *([cache_k] * n_pages), *([cache_v] * n_pages), q, kn, vn, fq, fk)


def _dsa_sample_kernel(pt_ref, *refs, n_pages, t, top_k):
    kp, vp, ip = refs[:n_pages], refs[n_pages:2 * n_pages], refs[2 * n_pages:3 * n_pages]
    q_ref, qi_ref, wi_ref, kn_ref, vn_ref, in_ref, o_ref = refs[3 * n_pages:]
    past = n_pages * PAGE_SIZE
    length = past + PAGE_SIZE

    qi = _head_rows(qi_ref[...]).astype(BF16)
    wi = wi_ref[...] * (IDX_DIM ** -0.5)
    i_blocks = [p[...].astype(BF16) for p in ip] + [_pad_rows(in_ref[...], PAGE_SIZE).astype(BF16)]
    s = jnp.concatenate([_nt_dot(qi, ib) for ib in i_blocks], axis=1)
    score = jnp.zeros((t, length), F32)
    for h in range(IDX_HEADS):
        score = score + jnp.maximum(s[h * t:(h + 1) * t], 0.0) * wi[:, h:h + 1]
    score = jnp.where(score == 0.0, 0.0, score)
    pos = lax.broadcasted_iota(I32, (t, length), 1)
    visible = pos <= past + lax.broadcasted_iota(I32, (t, length), 0)
    key = _sort_key(jnp.where(visible, score, -jnp.inf))
    thr, jmax = _topk_select(key, top_k, int(math.ceil(math.log2(length))))
    sel = (key > thr) | ((key == thr) & (pos <= jmax))
    bias = jnp.where(sel & (key > _neg_inf_key()), 0.0, NEG)

    qbd = _q_block_diag(q_ref[...] * (HEAD_DIM ** -0.5))
    k_blocks = [p[...].astype(BF16) for p in kp] + [_pad_rows(kn_ref[...], PAGE_SIZE).astype(BF16)]
    v_blocks = [p[...].astype(BF16) for p in vp] + [_pad_rows(vn_ref[...], PAGE_SIZE).astype(BF16)]
    logits = jnp.concatenate([_nt_dot(qbd, kb) for kb in k_blocks], axis=1)
    logits = logits + jnp.concatenate([bias] * DSA_HEADS, axis=0)
    o_ref[...] = _softmax_pv(logits, v_blocks, t)


def _dsa_sample(page_table, cache_k, cache_v, cache_i, q, qi, wi, kn, vn, kin, l):
    r, n_pages = page_table.shape
    t = q.shape[1]
    top_k = min(TOPK_MAX, (n_pages * PAGE_SIZE + t) // 4)
    req = lambda a, b: pl.BlockSpec((None, a, b), lambda bi, pt: (bi, 0, 0))
    grid_spec = pltpu.PrefetchScalarGridSpec(
        num_scalar_prefetch=1,
        grid=(r,),
        in_specs=(_page_specs(n_pages, PAGE_SIZE, KV_WIDTH, l) * 2 + _page_specs(n_pages, PAGE_SIZE, IDX_DIM, l)
                  + [req(t, Q_WIDTH), req(t, Q_WIDTH), req(t, IDX_HEADS), req(t, KV_WIDTH), req(t, KV_WIDTH),
                     req(t, IDX_DIM)]),
        out_specs=req(t, Q_WIDTH),
    )
    return pl.pallas_call(
        functools.partial(_dsa_sample_kernel, n_pages=n_pages, t=t, top_k=top_k),
        grid_spec=grid_spec,
        out_shape=jax.ShapeDtypeStruct((r, t, Q_WIDTH), F32),
        compiler_params=_params("parallel"),
        name="dsa_sample",
    )(page_table.reshape(-1), *([cache_k] * n_pages), *([cache_v] * n_pages), *([cache_i] * n_pages),
      q, qi, wi, kn, vn, kin)


def _merge_kernel(h_ref, od_ref, of_ref, zb_ref, z0_ref, z1_ref, z2_ref, gpre_ref, gpost_ref, cw_ref,
                  wg_ref, wbd_ref, wbc_ref, wbf_ref, wo_ref, o_ref):
    h = h_ref[...]
    d = h.shape[1]
    xn = _rms(h, gpre_ref[...]).astype(BF16)
    cw = cw_ref[...]
    conv = cw[0:1] * z0_ref[...] + cw[1:2] * z1_ref[...] + cw[2:3] * z2_ref[...]
    branches = ((od_ref[...], wbd_ref), (zb_ref[...] * conv, wbc_ref), (of_ref[...], wbf_ref))
    merged = jnp.zeros_like(h)
    for n, (val, w_ref) in enumerate(branches):
        gate = jax.nn.sigmoid(jnp.dot(xn, wg_ref[:, n * d:(n + 1) * d], preferred_element_type=F32))
        merged = merged + gate * jnp.dot(val.astype(BF16), w_ref[...], preferred_element_type=F32)
    m = jnp.dot(merged.astype(BF16), wo_ref[...], preferred_element_type=F32)
    o_ref[...] = h + _rms(m, gpost_ref[...])


def _merge(h, od, of, zb, z0, z1, z2, g_pre, g_post, conv_w, wg, wbd, wbc, wbf, wo, l, *, tm):
    n, d = h.shape
    tok = lambda w: pl.BlockSpec((tm, w), lambda i: (i, 0))
    lay = lambda a, b: pl.BlockSpec((None, a, b), lambda i: (l, 0, 0))
    return pl.pallas_call(
        _merge_kernel,
        grid=(n // tm,),
        in_specs=[tok(d), tok(Q_WIDTH), tok(Q_WIDTH), tok(CONV_WIDTH), tok(CONV_WIDTH), tok(CONV_WIDTH),
                  tok(CONV_WIDTH), lay(1, d), lay(1, d), lay(CONV_K, CONV_WIDTH),
                  lay(d, N_BRANCHES * d), lay(Q_WIDTH, d), lay(CONV_WIDTH, d), lay(Q_WIDTH, d), lay(d, d)],
        out_specs=tok(d),
        out_shape=jax.ShapeDtypeStruct((n, d), F32),
        compiler_params=_params("parallel"),
        name="merge",
    )(h, od, of, zb, z0, z1, z2, g_pre, g_post, conv_w, wg, wbd, wbc, wbf, wo)


def _rope_tables(pos):
    half = ROPE_DIM // 2
    inv = jnp.power(jnp.float32(ROPE_THETA), -jnp.arange(half, dtype=jnp.float32) * (2.0 / ROPE_DIM))
    ang = pos.astype(jnp.float32)[:, None] * inv[None, :]
    cos, sin = jnp.cos(ang), jnp.sin(ang)
    n = pos.shape[0]
    ones = jnp.ones((n, HEAD_DIM - ROPE_DIM), F32)
    zeros_r = jnp.zeros((n, HEAD_DIM - ROPE_DIM), F32)
    zeros_h = jnp.zeros((n, half), F32)
    cos_t = jnp.concatenate([cos, cos, ones], axis=1)
    sa_t = jnp.concatenate([-sin, zeros_h, zeros_r], axis=1)
    sb_t = jnp.concatenate([zeros_h, sin, zeros_r], axis=1)
    rep = LANES // HEAD_DIM
    return tuple(jnp.tile(x, (1, rep)) for x in (cos_t, sa_t, sb_t))


def _pick(n, prefs):
    for p in prefs:
        if n % p == 0:
            return p
    return n


def kernel(x_prompt, x_sample, cache_dsa_k, cache_dsa_v, cache_idx_k, cache_fox_k, cache_fox_v, cache_fox_logf,
           state_conv, page_table, g_ffn1_pre, g_ffn1_post, w_ffn1_gu, w_ffn1_dn, g_mix_pre, g_mix_post, w_in,
           b_forget, conv_w, w_br_dsa, w_br_conv, w_br_fox, w_out, g_ffn2_pre, g_ffn2_post, w_ffn2_gu, w_ffn2_dn):
    bsz, seq, d = x_prompt.shape
    r, ts, _ = x_sample.shape
    depth = w_in.shape[0]
    n_pool = cache_dsa_k.shape[1]
    n_pages = page_table.shape[1]
    past = n_pages * PAGE_SIZE
    n_p, n_s = bsz * seq, r * ts
    n = n_p + n_s
    dff = w_ffn1_dn.shape[1]

    tm = _pick(math.gcd(n_p, n_s), (512, 256, 128, 64, 32, 16, 8))
    tf = _pick(dff, (1408, 1024, 512, 256, 128))
    tq = _pick(seq, (128,))
    tk = _pick(seq, (512, 256, 128))

    sizes = (Q_WIDTH, KV_WIDTH, KV_WIDTH, IDX_HEADS * IDX_DIM, IDX_DIM, IDX_HEADS, CONV_WIDTH, CONV_WIDTH,
             CONV_WIDTH, Q_WIDTH, KV_WIDTH, KV_WIDTH, FOX_HEADS, N_BRANCHES * d)
    offs = [0]
    for s_ in sizes:
        offs.append(offs[-1] + s_)
    col = lambda i: w_in[:, :, offs[i]:offs[i + 1]]
    (c_qd, c_kd, c_vd, c_qi, c_ki, c_wi, c_zb, c_zc, c_zx, c_qf, c_kf, c_vf, c_fl, c_g) = [col(i) for i in range(14)]
    ki_pad = jnp.concatenate([c_ki, jnp.zeros((depth, d, LANES - IDX_DIM), F32)], axis=2)
    w_rope = jnp.concatenate([c_qd, c_kd, c_qi, ki_pad], axis=2).astype(BF16)
    w_plain = jnp.concatenate([c_vd, c_zb, c_qf, c_kf, c_vf], axis=2).astype(BF16)
    w_z = jnp.concatenate([c_zc, c_zx], axis=2).astype(BF16)
    w_wi, w_fl, w_g = c_wi.astype(BF16), c_fl.astype(BF16), c_g.astype(BF16)
    w1gu, w1dn = w_ffn1_gu.astype(BF16), w_ffn1_dn.astype(BF16)
    w2gu, w2dn = w_ffn2_gu.astype(BF16), w_ffn2_dn.astype(BF16)
    wbd, wbc, wbf, wo = (w.astype(BF16) for w in (w_br_dsa, w_br_conv, w_br_fox, w_out))
    row = lambda g: g.reshape(depth, 1, -1)
    g1pre, g1post, gmpre, gmpost, g2pre, g2post, bfg = (
        row(g) for g in (g_ffn1_pre, g_ffn1_post, g_mix_pre, g_mix_post, g_ffn2_pre, g_ffn2_post, b_forget))

    pos = jnp.concatenate([jnp.tile(jnp.arange(seq, dtype=I32), bsz),
                           jnp.tile(past + jnp.arange(ts, dtype=I32), r)])
    cos_t, sa_t, sb_t = _rope_tables(pos)

    ck_d = cache_dsa_k.reshape(depth, n_pool, PAGE_SIZE, KV_WIDTH)
    cv_d = cache_dsa_v.reshape(depth, n_pool, PAGE_SIZE, KV_WIDTH)
    ck_f = cache_fox_k.reshape(depth, n_pool, PAGE_SIZE, KV_WIDTH)
    cv_f = cache_fox_v.reshape(depth, n_pool, PAGE_SIZE, KV_WIDTH)
    clf_t = jnp.swapaxes(cache_fox_logf, 2, 3)

    x = jnp.concatenate([x_prompt.reshape(n_p, d), x_sample.reshape(n_s, d)], axis=0)
    rows_p = [[] for _ in range(7)]
    rows_s = [[] for _ in range(7)]
    for l in range(depth):
        h = _ffn(x, g1pre, g1post, w1gu, w1dn, l, tm=tm, tf=tf)
        qd, kd, qi, ki, vd, zb, qf, kf, vf, z, wi, lf = _inproj(
            h, gmpre, cos_t, sa_t, sb_t, bfg, w_rope, w_plain, w_z, w_wi, w_fl, l, tm=tm)

        pr = lambda a: a[:n_p].reshape(bsz, seq, a.shape[1])
        sm = lambda a: a[n_p:].reshape(r, ts, a.shape[1])

        o_dsa_p = _dsa_prompt(pr(qd), pr(qi), pr(wi), pr(kd), pr(vd), pr(ki), tq=tq, tk=tk)
        f_k = _cumsum_prompt(jnp.swapaxes(pr(lf), 1, 2))
        f_q = jnp.swapaxes(f_k, 1, 2)
        f_kc = f_k.reshape(bsz, FOX_HEADS, seq // tk, tk).transpose(0, 2, 1, 3)
        o_fox_p = _fox_prompt(pr(qf), pr(kf), pr(vf), f_q, f_kc, tq=tq, tk=tk)

        o_dsa_s = _dsa_sample(page_table, ck_d, cv_d, cache_idx_k, sm(qd), sm(qi), sm(wi), sm(kd), sm(vd), sm(ki), l)
        lf_new_t = jnp.pad(jnp.swapaxes(sm(lf), 1, 2), ((0, 0), (0, 0), (0, LANES - ts)))
        f_all = _cumsum_sample(page_table, clf_t, lf_new_t, l)
        f_q_s = jnp.swapaxes(f_all[:, :, past:past + ts], 1, 2)
        o_fox_s = _fox_sample(page_table, ck_f, cv_f, sm(qf), sm(kf), sm(vf), f_q_s, f_all, l)

        z_p, z_s = pr(z), sm(z)
        zp_p = jnp.concatenate([jnp.zeros((bsz, CONV_K - 1, CONV_WIDTH), F32), z_p], axis=1)
        zp_s = jnp.concatenate([state_conv[l].astype(F32), z_s], axis=1)
        taps = [jnp.concatenate([zp_p[:, j:j + seq].reshape(n_p, CONV_WIDTH),
                                 zp_s[:, j:j + ts].reshape(n_s, CONV_WIDTH)], axis=0) for j in range(CONV_K)]

        o_dsa = jnp.concatenate([o_dsa_p.reshape(n_p, Q_WIDTH), o_dsa_s.reshape(n_s, Q_WIDTH)], axis=0)
        o_fox = jnp.concatenate([o_fox_p.reshape(n_p, Q_WIDTH), o_fox_s.reshape(n_s, Q_WIDTH)], axis=0)
        h = _merge(h, o_dsa, o_fox, zb, taps[0], taps[1], taps[2], gmpre, gmpost, conv_w,
                   w_g, wbd, wbc, wbf, wo, l, tm=tm)
        x = _ffn(h, g2pre, g2post, w2gu, w2dn, l, tm=tm, tf=tf)

        heads = lambda a: a.reshape(*a.shape[:-1], DSA_KV_HEADS, HEAD_DIM)
        for dst, take, zp, length in ((rows_p, pr, zp_p, seq), (rows_s, sm, zp_s, ts)):
            dst[0].append(heads(take(kd)))
            dst[1].append(heads(take(vd)))
            dst[2].append(take(ki))
            dst[3].append(heads(take(kf)))
            dst[4].append(heads(take(vf)))
            dst[5].append(take(lf))
            dst[6].append(zp[:, length:length + CONV_K - 1])

    sp = [jnp.stack(a, axis=0) for a in rows_p]
    ss = [jnp.stack(a, axis=0) for a in rows_s]
    return (x[:n_p].reshape(bsz, seq, d), x[n_p:].reshape(r, ts, d),
            sp[0], sp[1], sp[2], sp[3], sp[4], sp[5], sp[6],
            ss[0], ss[1], ss[2], ss[3], ss[4], ss[5], ss[6])
```

```python
import functools
import math

import jax
import jax.numpy as jnp
from jax import lax
from jax.experimental import pallas as pl
from jax.experimental.pallas import tpu as pltpu

HEAD_DIM = 64
DSA_HEADS = 8
DSA_KV_HEADS = 4
IDX_HEADS = 8
IDX_DIM = 64
TOPK_MAX = 256
CONV_WIDTH = 512
CONV_K = 3
FOX_HEADS = 8
FOX_KV_HEADS = 4
ROPE_THETA = 500000.0
ROPE_DIM = HEAD_DIM // 4
NORM_EPS = 1e-6
N_BRANCHES = 3
PAGE_SIZE = 128

LANES = 128
SUBLANES = 8
KV_WIDTH = DSA_KV_HEADS * HEAD_DIM
Q_WIDTH = DSA_HEADS * HEAD_DIM
VMEM_LIMIT = 56 * 1024 * 1024

F32 = jnp.float32
BF16 = jnp.bfloat16
I32 = jnp.int32
NEG = -1e30
INT_MIN = -(2 ** 31)
NEG_INF_KEY = INT_MIN + 0x7FFFFF
THR_ROW_GROUPS = 2

assert DSA_HEADS == FOX_HEADS and DSA_KV_HEADS == FOX_KV_HEADS
assert DSA_HEADS // DSA_KV_HEADS == 2 and 2 * HEAD_DIM == LANES


def _rms(x, g):
    return x * lax.rsqrt(jnp.mean(x * x, axis=-1, keepdims=True) + NORM_EPS) * g


def _nt_dot(a, b):
    return lax.dot_general(a, b, (((1,), (1,)), ((), ())), preferred_element_type=F32)


def _dot(a, b):
    return jnp.dot(a, b, preferred_element_type=F32)


def _params(*sem):
    return pltpu.CompilerParams(dimension_semantics=sem, vmem_limit_bytes=VMEM_LIMIT)


def _ffn_kernel(x_ref, gpre_ref, gpost_ref, wa_ref, wb_ref, wd_ref, o_ref, xn_ref, acc_ref, *, nf):
    j = pl.program_id(1)

    @pl.when(j == 0)
    def _():
        xn_ref[...] = _rms(x_ref[...], gpre_ref[...]).astype(BF16)
        acc_ref[...] = jnp.zeros_like(acc_ref)

    xn = xn_ref[...]
    a = _dot(xn, wa_ref[...])
    b = _dot(xn, wb_ref[...])
    act = (a * jax.nn.sigmoid(a) * b).astype(BF16)
    acc_ref[...] += _dot(act, wd_ref[...])

    @pl.when(j == nf - 1)
    def _():
        o_ref[...] = x_ref[...] + 0.5 * _rms(acc_ref[...], gpost_ref[...])


def _ffn(x, g_pre, g_post, w_gu, w_dn, l, *, tm, tf):
    n, d = x.shape
    dff = w_dn.shape[1]
    nf = dff // tf
    return pl.pallas_call(
        functools.partial(_ffn_kernel, nf=nf),
        grid=(n // tm, nf),
        in_specs=[
            pl.BlockSpec((tm, d), lambda i, j: (i, 0)),
            pl.BlockSpec((None, 1, d), lambda i, j: (l, 0, 0)),
            pl.BlockSpec((None, 1, d), lambda i, j: (l, 0, 0)),
            pl.BlockSpec((None, d, tf), lambda i, j: (l, 0, j)),
            pl.BlockSpec((None, d, tf), lambda i, j: (l, 0, j + nf)),
            pl.BlockSpec((None, tf, d), lambda i, j: (l, j, 0)),
        ],
        out_specs=pl.BlockSpec((tm, d), lambda i, j: (i, 0)),
        out_shape=jax.ShapeDtypeStruct((n, d), F32),
        scratch_shapes=[pltpu.VMEM((tm, d), BF16), pltpu.VMEM((tm, d), F32)],
        compiler_params=_params("parallel", "arbitrary"),
        name="ffn",
    )(x, g_pre, g_post, w_gu, w_gu, w_dn)


ROPE_COLS = Q_WIDTH + KV_WIDTH + Q_WIDTH + LANES
PLAIN_COLS = KV_WIDTH + CONV_WIDTH + Q_WIDTH + 2 * KV_WIDTH


def _log_sigmoid(x):
    return jnp.minimum(x, 0.0) - jnp.log1p(jnp.exp(-jnp.abs(x)))


def _inproj_kernel(x_ref, g_ref, cos_ref, sa_ref, sb_ref, bf_ref, wr_ref, wp_ref, wz_ref, wwi_ref, wfl_ref,
                   qd_o, kd_o, qi_o, ki_o, vd_o, zb_o, qf_o, kf_o, vf_o, z_o, wi_o, lf_o):
    xn = _rms(x_ref[...], g_ref[...]).astype(BF16)
    cos, sa, sb = cos_ref[...], sa_ref[...], sb_ref[...]

    r = _dot(xn, wr_ref[...])
    outs = []
    for c in range(ROPE_COLS // LANES):
        v = r[:, c * LANES:(c + 1) * LANES]
        up = pltpu.roll(v, LANES - ROPE_DIM // 2, axis=1)
        dn = pltpu.roll(v, ROPE_DIM // 2, axis=1)
        outs.append(v * cos + up * sa + dn * sb)
    nq = Q_WIDTH // LANES
    nk = KV_WIDTH // LANES
    qd_o[...] = jnp.concatenate(outs[:nq], axis=1)
    kd_o[...] = jnp.concatenate(outs[nq:nq + nk], axis=1)
    qi_o[...] = jnp.concatenate(outs[nq + nk:2 * nq + nk], axis=1)
    ki_o[...] = outs[2 * nq + nk][:, :IDX_DIM]

    p = _dot(xn, wp_ref[...])
    o = 0
    for ref, w in ((vd_o, KV_WIDTH), (zb_o, CONV_WIDTH), (qf_o, Q_WIDTH), (kf_o, KV_WIDTH), (vf_o, KV_WIDTH)):
        ref[...] = p[:, o:o + w]
        o += w

    zz = _dot(xn, wz_ref[...])
    z_o[...] = zz[:, :CONV_WIDTH] * zz[:, CONV_WIDTH:]

    wi_o[...] = _dot(xn, wwi_ref[...]) * (IDX_HEADS ** -0.5)
    lf_o[...] = _log_sigmoid(_dot(xn, wfl_ref[...]) + bf_ref[...])


def _inproj(x, g, cos, sa, sb, b_forget, wr, wp, wz, wwi, wfl, l, *, tm):
    n, d = x.shape
    tok = lambda w: pl.BlockSpec((tm, w), lambda i: (i, 0))
    lay = lambda a, b: pl.BlockSpec((None, a, b), lambda i: (l, 0, 0))
    widths = (Q_WIDTH, KV_WIDTH, Q_WIDTH, IDX_DIM, KV_WIDTH, CONV_WIDTH, Q_WIDTH, KV_WIDTH, KV_WIDTH,
              CONV_WIDTH, IDX_HEADS, FOX_HEADS)
    return pl.pallas_call(
        _inproj_kernel,
        grid=(n // tm,),
        in_specs=[tok(d), lay(1, d), tok(LANES), tok(LANES), tok(LANES), lay(1, FOX_HEADS),
                  lay(d, ROPE_COLS), lay(d, PLAIN_COLS), lay(d, 2 * CONV_WIDTH),
                  lay(d, IDX_HEADS), lay(d, FOX_HEADS)],
        out_specs=[tok(w) for w in widths],
        out_shape=[jax.ShapeDtypeStruct((n, w), F32) for w in widths],
        compiler_params=_params("parallel"),
        name="inproj",
    )(x, g, cos, sa, sb, b_forget, wr, wp, wz, wwi, wfl)


def _block_cumsum(x, n_blocks):
    rows = n_blocks * SUBLANES
    padded = -(-rows // LANES) * LANES
    if padded != rows:
        x = jnp.concatenate([x, jnp.zeros((padded - rows, LANES), F32)], axis=0)
    hi = functools.partial(jnp.dot, preferred_element_type=F32, precision=lax.Precision.HIGHEST)
    r0 = lax.broadcasted_iota(I32, (LANES, LANES), 0)
    c0 = lax.broadcasted_iota(I32, (LANES, LANES), 1)
    within = hi(x, jnp.where(r0 <= c0, 1.0, 0.0).astype(F32))
    totals = jnp.broadcast_to(within[:, LANES - 1:LANES], (padded, LANES))
    dist = lax.broadcasted_iota(I32, (padded, padded), 0) - lax.broadcasted_iota(I32, (padded, padded), 1)
    earlier = jnp.where((dist > 0) & ((dist & (SUBLANES - 1)) == 0), 1.0, 0.0).astype(F32)
    full = within + hi(earlier, totals)
    return [full[b * SUBLANES:(b + 1) * SUBLANES] for b in range(n_blocks)]


def _cumsum_kernel(x_ref, o_ref, *, n_blocks):
    o_ref[...] = jnp.concatenate(_block_cumsum(x_ref[...], n_blocks), axis=1)


def _cumsum_prompt(lf_blocks):
    b, rows, _ = lf_blocks.shape
    n_blocks = rows // FOX_HEADS
    t = n_blocks * LANES
    return pl.pallas_call(
        functools.partial(_cumsum_kernel, n_blocks=n_blocks),
        grid=(b,),
        in_specs=[pl.BlockSpec((None, rows, LANES), lambda i: (i, 0, 0))],
        out_specs=pl.BlockSpec((None, FOX_HEADS, t), lambda i: (i, 0, 0)),
        out_shape=jax.ShapeDtypeStruct((b, FOX_HEADS, t), F32),
        compiler_params=_params("parallel"),
        name="cumsum_prompt",
    )(lf_blocks)


def _q_block_diag(q):
    t = q.shape[0]
    lane = lax.broadcasted_iota(I32, (t, LANES), 1)
    zeros = jnp.zeros((t, LANES), F32)
    blocks = []
    for h in range(DSA_HEADS):
        g, r = divmod(h, 2)
        src = q[:, g * LANES:(g + 1) * LANES]
        if r != g % 2:
            src = pltpu.roll(src, HEAD_DIM, axis=1)
        keep = (lane < HEAD_DIM) if g % 2 == 0 else (lane >= HEAD_DIM)
        m = jnp.where(keep, src, 0.0)
        blocks.append(jnp.concatenate([m, zeros] if g // 2 == 0 else [zeros, m], axis=1))
    return jnp.concatenate(blocks, axis=0).astype(BF16)


def _head_rows(q):
    return jnp.concatenate([q[:, h * IDX_DIM:(h + 1) * IDX_DIM] for h in range(IDX_HEADS)], axis=0)


def _extract_heads(acc, t):
    outs = []
    for h in range(DSA_HEADS):
        g = h // 2
        outs.append(acc[h * t:(h + 1) * t, g * HEAD_DIM:(g + 1) * HEAD_DIM])
    return jnp.concatenate(outs, axis=1)


def _sort_key(score):
    bits = pltpu.bitcast(score, I32)
    return jnp.where(bits < 0, bits ^ 0x7FFFFFFF, bits)


def _indexer_score(s, wi, t):
    score = jnp.zeros((t, s.shape[1]), F32)
    for h in range(s.shape[0] // t):
        score = score + jnp.maximum(s[h * t:(h + 1) * t], 0.0) * wi[:, h:h + 1]
    return jnp.where(score == 0.0, 0.0, score)


def _fold_lanes(x):
    out = x[:, :LANES]
    for j in range(1, x.shape[1] // LANES):
        out = out + x[:, j * LANES:(j + 1) * LANES]
    return out


LOG2E = 1.4426950408889634


def _online_softmax_step(qbd, kt, v, bias_fn, m_ref, l_ref, acc_ref, tq):
    tk = kt.shape[1]
    group = DSA_HEADS // DSA_KV_HEADS
    for g in range(DSA_HEADS // group):
        h0, h1 = g * group, (g + 1) * group
        s3 = _dot(qbd[h0 * tq:h1 * tq], kt).reshape(group, tq, tk) + bias_fn(h0, h1)
        m_old = m_ref[h0:h1]
        m_new = jnp.maximum(m_old, jnp.max(s3, axis=2, keepdims=True))
        p = jnp.exp2(s3 - m_new)
        alpha = jnp.exp2(m_old - m_new)
        l_ref[h0:h1] = alpha * l_ref[h0:h1] + jnp.sum(p, axis=2, keepdims=True)
        m_ref[h0:h1] = m_new
        pv = _dot(p.reshape(group * tq, tk).astype(BF16), v)
        acc_ref[h0:h1] = alpha * acc_ref[h0:h1] + pv.reshape(group, tq, KV_WIDTH)


def _softmax_scratch(tq):
    return [pltpu.VMEM((DSA_HEADS, tq, 1), F32), pltpu.VMEM((DSA_HEADS, tq, 1), F32),
            pltpu.VMEM((DSA_HEADS, tq, KV_WIDTH), F32)]


def _softmax_result(l_ref, acc_ref, tq):
    return _extract_heads((acc_ref[...] / l_ref[...]).reshape(DSA_HEADS * tq, KV_WIDTH), tq)


def _softmax_init(m_ref, l_ref, acc_ref):
    m_ref[...] = jnp.full_like(m_ref, NEG)
    l_ref[...] = jnp.zeros_like(l_ref)
    acc_ref[...] = jnp.zeros_like(acc_ref)


def _fox_prompt_kernel(q_ref, kt_ref, v_ref, fq_ref, fk_ref, o_ref, m_ref, l_ref, acc_ref, *, tq, tk):
    i = pl.program_id(1)
    qbd = _q_block_diag(q_ref[...] * (HEAD_DIM ** -0.5 * LOG2E))
    fq = fq_ref[...] * LOG2E
    fq_rep = jnp.stack([jnp.broadcast_to(fq[:, h:h + 1], (tq, LANES)) for h in range(FOX_HEADS)], axis=0)
    _softmax_init(m_ref, l_ref, acc_ref)
    qpos = i * tq + lax.broadcasted_iota(I32, (tq, tk), 0)
    kiota = lax.broadcasted_iota(I32, (tq, tk), 1)

    def chunk(c, carry, *, masked):
        start = pl.multiple_of(c * tk, tk)
        v = v_ref[pl.ds(start, tk), :].astype(BF16)
        fk = fk_ref[c] * LOG2E
        fk3 = jnp.stack([fk[h:h + 1, :] for h in range(FOX_HEADS)], axis=0)
        causal = (kiota + c * tk) <= qpos

        def bias(h0, h1):
            b = jnp.concatenate([fq_rep[h0:h1]] * (tk // LANES), axis=2) - fk3[h0:h1]
            return jnp.where(causal[None], b, NEG) if masked else b

        _online_softmax_step(qbd, kt_ref[c].astype(BF16), v, bias, m_ref, l_ref, acc_ref, tq)
        return carry

    n_full = (i * tq + 1) // tk
    lax.fori_loop(0, n_full, functools.partial(chunk, masked=False), 0)
    lax.fori_loop(n_full, (i * tq + tq - 1) // tk + 1, functools.partial(chunk, masked=True), 0)
    o_ref[...] = _softmax_result(l_ref, acc_ref, tq)


def _fox_prompt(q, kt, v, fq, fk, bsz, seq, *, tq, tk):
    n = q.shape[0]
    nq, nk = seq // tq, seq // tk
    return pl.pallas_call(
        functools.partial(_fox_prompt_kernel, tq=tq, tk=tk),
        grid=(bsz, nq),
        in_specs=[
            pl.BlockSpec((tq, Q_WIDTH), lambda bi, i: (bi * nq + i, 0)),
            pl.BlockSpec((None, nk, KV_WIDTH, tk), lambda bi, i: (bi, 0, 0, 0)),
            pl.BlockSpec((seq, KV_WIDTH), lambda bi, i: (bi, 0)),
            pl.BlockSpec((None, tq, FOX_HEADS), lambda bi, i: (bi, i, 0)),
            pl.BlockSpec((None, nk, FOX_HEADS, tk), lambda bi, i: (bi, 0, 0, 0)),
        ],
        out_specs=pl.BlockSpec((tq, Q_WIDTH), lambda bi, i: (bi * nq + i, 0)),
        out_shape=jax.ShapeDtypeStruct((n, Q_WIDTH), F32),
        scratch_shapes=_softmax_scratch(tq),
        compiler_params=_params("parallel", "parallel"),
        name="fox_prompt",
    )(q, kt, v, fq, fk)


def _dsa_prompt_kernel(q_ref, qi_ref, wi_ref, kt_ref, v_ref, kit_ref, o_ref,
                       key_ref, thr_ref, j_ref, m_ref, l_ref, acc_ref, *, tq, tk, top_k, seq):
    i = pl.program_id(1)
    n_vis = (i * tq + tq - 1) // tk + 1
    qpos = i * tq + lax.broadcasted_iota(I32, (tq, tk), 0)
    kiota = lax.broadcasted_iota(I32, (tq, tk), 1)
    kf = float(top_k)

    qi = _head_rows(qi_ref[...]).astype(BF16)
    wi = wi_ref[...] * (IDX_DIM ** -0.5)

    def score_chunk(c, carry):
        kit = kit_ref[c].astype(BF16)
        half = IDX_HEADS // 2 * tq
        score = (_indexer_score(_dot(qi[:half], kit), wi[:, :IDX_HEADS // 2], tq)
                 + _indexer_score(_dot(qi[half:], kit), wi[:, IDX_HEADS // 2:], tq))
        score = jnp.where((kiota + c * tk) <= qpos, score, -jnp.inf)
        key_ref[c] = _sort_key(score)
        return carry

    lax.fori_loop(0, n_vis, score_chunk, 0)

    def count(pred):
        def body(c, cnt):
            return cnt + _fold_lanes(jnp.where(pred(key_ref[c], kiota + c * tk), 1.0, 0.0))
        return jnp.sum(lax.fori_loop(0, n_vis, body, jnp.zeros((tq, LANES), F32)), axis=1, keepdims=True)

    rows_g = tq // THR_ROW_GROUPS

    def search(n_chunks):
        def step(b, cands):
            bit = lax.shift_left(jnp.int32(1), 31 - b)
            out = []
            for r in range(THR_ROW_GROUPS):
                trial = cands[r] ^ bit
                cnt = jnp.zeros((rows_g, LANES), F32)
                for c in range(n_chunks):
                    key = key_ref[c, r * rows_g:(r + 1) * rows_g, :]
                    cnt = cnt + _fold_lanes(jnp.where(key >= trial, 1.0, 0.0))
                out.append(jnp.where(jnp.sum(cnt, axis=1, keepdims=True) >= kf, trial, cands[r]))
            return tuple(out)

        init = tuple(jnp.full((rows_g, 1), INT_MIN, I32) for _ in range(THR_ROW_GROUPS))
        return jnp.concatenate(lax.fori_loop(0, 32, step, init), axis=0)

    for n_chunks in range(1, seq // tk + 1):
        @pl.when(n_vis == n_chunks)
        def _():
            thr_ref[...] = search(n_chunks)

    thr = thr_ref[...]

    def gt_ge(c, cnt):
        key = key_ref[c]
        return (cnt[0] + _fold_lanes(jnp.where(key > thr, 1.0, 0.0)),
                cnt[1] + _fold_lanes(jnp.where(key >= thr, 1.0, 0.0)))

    zero = jnp.zeros((tq, LANES), F32)
    n_gt, n_ge = (jnp.sum(x, axis=1, keepdims=True) for x in lax.fori_loop(0, n_vis, gt_ge, (zero, zero)))
    need = kf - n_gt
    j_ref[...] = jnp.full_like(j_ref, seq)
    tied = jnp.where((n_ge > kf) & (thr != NEG_INF_KEY), 1.0, 0.0)

    @pl.when(jnp.max(tied) > 0.0)
    def _():
        def idx_step(b, lo):
            trial = lo | lax.shift_left(jnp.int32(1), int(math.log2(seq)) - 1 - b)
            c = count(lambda key, pos: jnp.where(key == thr, pos, seq) < trial)
            return jnp.where(c < need, trial, lo)

        j_ref[...] = lax.fori_loop(0, int(math.log2(seq)), idx_step, jnp.zeros((tq, 1), I32))

    jmax = j_ref[...]

    qbd = _q_block_diag(q_ref[...] * (HEAD_DIM ** -0.5 * LOG2E))
    _softmax_init(m_ref, l_ref, acc_ref)

    def attn_chunk(c, carry):
        start = pl.multiple_of(c * tk, tk)
        v = v_ref[pl.ds(start, tk), :].astype(BF16)
        key = key_ref[c]
        tie_bias = jnp.where(jnp.where(key == thr, kiota + c * tk, seq + 1) <= jmax, 0.0, NEG)
        bias = jnp.where(key > NEG_INF_KEY, jnp.where(key > thr, 0.0, tie_bias), NEG)
        _online_softmax_step(qbd, kt_ref[c].astype(BF16), v, lambda h0, h1: bias[None], m_ref, l_ref, acc_ref, tq)
        return carry

    lax.fori_loop(0, n_vis, attn_chunk, 0)
    o_ref[...] = _softmax_result(l_ref, acc_ref, tq)


def _dsa_prompt(q, qi, wi, kt, v, kit, bsz, seq, *, tq, tk):
    n = q.shape[0]
    nq, nk = seq // tq, seq // tk
    top_k = min(TOPK_MAX, seq // 4)
    tokq = lambda w: pl.BlockSpec((tq, w), lambda bi, i: (bi * nq + i, 0))
    return pl.pallas_call(
        functools.partial(_dsa_prompt_kernel, tq=tq, tk=tk, top_k=top_k, seq=seq),
        grid=(bsz, nq),
        in_specs=[
            tokq(Q_WIDTH), tokq(Q_WIDTH), tokq(IDX_HEADS),
            pl.BlockSpec((None, nk, KV_WIDTH, tk), lambda bi, i: (bi, 0, 0, 0)),
            pl.BlockSpec((seq, KV_WIDTH), lambda bi, i: (bi, 0)),
            pl.BlockSpec((None, nk, IDX_DIM, tk), lambda bi, i: (bi, 0, 0, 0)),
        ],
        out_specs=tokq(Q_WIDTH),
        out_shape=jax.ShapeDtypeStruct((n, Q_WIDTH), F32),
        scratch_shapes=[pltpu.VMEM((nk, tq, tk), I32), pltpu.VMEM((tq, 1), I32), pltpu.VMEM((tq, 1), I32)]
        + _softmax_scratch(tq),
        compiler_params=_params("parallel", "parallel"),
        name="dsa_prompt",
    )(q, qi, wi, kt, v, kit)


def _page_specs(n_pages, rows, l):
    return [pl.BlockSpec((None, None, rows, PAGE_SIZE),
                         functools.partial(lambda b, pt, p: (l, pt[b * n_pages + p], 0, 0), p=p))
            for p in range(n_pages)]


def _pad_rows(x, rows):
    return jnp.concatenate([x, jnp.zeros((rows - x.shape[0], x.shape[1]), x.dtype)], axis=0)


def _paged_logits(lhs, pages_t, new_rows):
    blocks = [_dot(lhs, p[...].astype(BF16)) for p in pages_t]
    blocks.append(_nt_dot(lhs, _pad_rows(new_rows, PAGE_SIZE).astype(BF16)))
    return jnp.concatenate(blocks, axis=1)


def _softmax_pv(logits, v_pages_t, v_new, t):
    m = jnp.max(logits, axis=1, keepdims=True)
    p = jnp.exp(logits - m)
    l = jnp.sum(p, axis=1, keepdims=True)
    pb = p.astype(BF16)
    acc = jnp.zeros((logits.shape[0], KV_WIDTH), F32)
    for n, vp in enumerate(v_pages_t):
        acc = acc + _nt_dot(pb[:, n * PAGE_SIZE:(n + 1) * PAGE_SIZE], vp[...].astype(BF16))
    n = len(v_pages_t)
    acc = acc + _dot(pb[:, n * PAGE_SIZE:(n + 1) * PAGE_SIZE], _pad_rows(v_new, PAGE_SIZE).astype(BF16))
    return _extract_heads(acc / l, t)


def _visible(t, length, past):
    pos = lax.broadcasted_iota(I32, (t, length), 1)
    return pos, pos <= past + lax.broadcasted_iota(I32, (t, length), 0)


def _fox_sample_kernel(pt_ref, *refs, n_pages, t):
    kp, vp, lp = refs[:n_pages], refs[n_pages:2 * n_pages], refs[2 * n_pages:3 * n_pages]
    q_ref, kn_ref, vn_ref, lfn_ref, _, o_ref = refs[3 * n_pages:]
    past = n_pages * PAGE_SIZE
    length = past + PAGE_SIZE

    f_tiles = _block_cumsum(jnp.concatenate([p[...] for p in lp] + [lfn_ref[...]], axis=0), n_pages + 1)
    fk = jnp.concatenate(f_tiles, axis=1)
    lane = lax.broadcasted_iota(I32, (t, LANES), 1)
    diag = lane == lax.broadcasted_iota(I32, (t, LANES), 0)

    qbd = _q_block_diag(q_ref[...] * (HEAD_DIM ** -0.5))
    s = _paged_logits(qbd, kp, kn_ref[...])
    _, visible = _visible(t, length, past)
    rows = []
    for h in range(FOX_HEADS):
        f_new = jnp.broadcast_to(f_tiles[n_pages][h:h + 1, :], (t, LANES))
        fq = jnp.sum(jnp.where(diag, f_new, 0.0), axis=1, keepdims=True)
        sh = s[h * t:(h + 1) * t] + (fq - fk[h:h + 1, :])
        rows.append(jnp.where(visible, sh, NEG))
    o_ref[...] = _softmax_pv(jnp.concatenate(rows, axis=0), vp, vn_ref[...], t)


def _fox_sample(page_table, cache_kt, cache_vt, cache_lft, q, kn, vn, lf_new_t, o_prev, n_p, l):
    r, n_pages = page_table.shape
    t = (q.shape[0] - n_p) // r
    off = n_p // t
    req = lambda w: pl.BlockSpec((t, w), lambda bi, pt: (off + bi, 0))
    grid_spec = pltpu.PrefetchScalarGridSpec(
        num_scalar_prefetch=1,
        grid=(r,),
        in_specs=(_page_specs(n_pages, KV_WIDTH, l) * 2 + _page_specs(n_pages, FOX_HEADS, l)
                  + [req(Q_WIDTH), req(KV_WIDTH), req(KV_WIDTH),
                     pl.BlockSpec((None, FOX_HEADS, LANES), lambda bi, pt: (bi, 0, 0)),
                     pl.BlockSpec(memory_space=pl.ANY)]),
        out_specs=req(Q_WIDTH),
    )
    n_in = 1 + 3 * n_pages + 5
    return pl.pallas_call(
        functools.partial(_fox_sample_kernel, n_pages=n_pages, t=t),
        grid_spec=grid_spec,
        out_shape=jax.ShapeDtypeStruct(o_prev.shape, F32),
        input_output_aliases={n_in - 1: 0},
        compiler_params=_params("parallel"),
        name="fox_sample",
    )(page_table.reshape(-1), *([cache_kt] * n_pages), *([cache_vt] * n_pages), *([cache_lft] * n_pages),
      q, kn, vn, lf_new_t, o_prev)


def _sample_topk(key, pos, top_k, length, j_ref):
    t = key.shape[0]
    kf = float(top_k)
    count = lambda mask: jnp.sum(jnp.where(mask, 1.0, 0.0), axis=1, keepdims=True)

    def digit_pass(cand, shift, n_trials):
        digit = jnp.zeros((t, 1), I32)
        for j in range(1, n_trials + 1):
            trial = cand + jnp.left_shift(jnp.int32(j), shift)
            digit = digit + jnp.where(count(key >= trial) >= kf, 1, 0)
        return cand + jnp.left_shift(digit, shift)

    cand = digit_pass(jnp.full((t, 1), INT_MIN, I32), jnp.int32(30), 3)
    thr = lax.fori_loop(0, 10, lambda b, c: digit_pass(c, 27 - 3 * b, 7), cand)

    need = kf - count(key > thr)
    n_ge = count(key >= thr)
    j_ref[...] = jnp.full_like(j_ref, length)
    tied = jnp.where((n_ge > kf) & (thr != NEG_INF_KEY), 1.0, 0.0)

    @pl.when(jnp.max(tied) > 0.0)
    def _():
        n_bits = int(math.ceil(math.log2(length)))
        tie_pos = jnp.where(key == thr, pos, length)

        def idx_step(b, lo):
            trial = lo | lax.shift_left(jnp.int32(1), n_bits - 1 - b)
            return jnp.where(count(tie_pos < trial) < need, trial, lo)

        j_ref[...] = lax.fori_loop(0, n_bits, idx_step, jnp.zeros((t, 1), I32))

    return thr, j_ref[...]


def _dsa_sample_kernel(pt_ref, *refs, n_pages, t, top_k):
    kp, vp, ip = refs[:n_pages], refs[n_pages:2 * n_pages], refs[2 * n_pages:3 * n_pages]
    q_ref, qi_ref, wi_ref, kn_ref, vn_ref, in_ref, _, o_ref, j_ref = refs[3 * n_pages:]
    past = n_pages * PAGE_SIZE
    length = past + PAGE_SIZE

    qi = _head_rows(qi_ref[...]).astype(BF16)
    score = _indexer_score(_paged_logits(qi, ip, in_ref[...]), wi_ref[...] * (IDX_DIM ** -0.5), t)
    pos, visible = _visible(t, length, past)
    key = _sort_key(jnp.where(visible, score, -jnp.inf))
    thr, jmax = _sample_topk(key, pos, top_k, length, j_ref)
    tie_bias = jnp.where(jnp.where(key == thr, pos, length + 1) <= jmax, 0.0, NEG)
    bias = jnp.where(key > NEG_INF_KEY, jnp.where(key > thr, 0.0, tie_bias), NEG)

    qbd = _q_block_diag(q_ref[...] * (HEAD_DIM ** -0.5))
    logits = _paged_logits(qbd, kp, kn_ref[...]) + jnp.concatenate([bias] * DSA_HEADS, axis=0)
    o_ref[...] = _softmax_pv(logits, vp, vn_ref[...], t)


def _dsa_sample(page_table, cache_kt, cache_vt, cache_it, q, qi, wi, kn, vn, kin, o_prev, n_p, l):
    r, n_pages = page_table.shape
    t = (q.shape[0] - n_p) // r
    off = n_p // t
    top_k = min(TOPK_MAX, (n_pages * PAGE_SIZE + t) // 4)
    req = lambda w: pl.BlockSpec((t, w), lambda bi, pt: (off + bi, 0))
    grid_spec = pltpu.PrefetchScalarGridSpec(
        num_scalar_prefetch=1,
        grid=(r,),
        in_specs=(_page_specs(n_pages, KV_WIDTH, l) * 2 + _page_specs(n_pages, IDX_DIM, l)
                  + [req(Q_WIDTH), req(Q_WIDTH), req(IDX_HEADS), req(KV_WIDTH), req(KV_WIDTH), req(IDX_DIM),
                     pl.BlockSpec(memory_space=pl.ANY)]),
        out_specs=req(Q_WIDTH),
        scratch_shapes=[pltpu.VMEM((t, 1), I32)],
    )
    n_in = 1 + 3 * n_pages + 7
    return pl.pallas_call(
        functools.partial(_dsa_sample_kernel, n_pages=n_pages, t=t, top_k=top_k),
        grid_spec=grid_spec,
        out_shape=jax.ShapeDtypeStruct(o_prev.shape, F32),
        input_output_aliases={n_in - 1: 0},
        compiler_params=_params("parallel"),
        name="dsa_sample",
    )(page_table.reshape(-1), *([cache_kt] * n_pages), *([cache_vt] * n_pages), *([cache_it] * n_pages),
      q, qi, wi, kn, vn, kin, o_prev)


def _merge_kernel(h_ref, od_ref, of_ref, zb_ref, z0_ref, z1_ref, z2_ref, gpre_ref, gpost_ref, cw_ref,
                  wg_ref, wbd_ref, wbc_ref, wbf_ref, wo_ref, o_ref):
    h = h_ref[...]
    d = h.shape[1]
    xn = _rms(h, gpre_ref[...]).astype(BF16)
    cw = cw_ref[...]
    conv = cw[0:1] * z0_ref[...] + cw[1:2] * z1_ref[...] + cw[2:3] * z2_ref[...]
    branches = ((od_ref[...], wbd_ref), (zb_ref[...] * conv, wbc_ref), (of_ref[...], wbf_ref))
    merged = jnp.zeros_like(h)
    for n, (val, w_ref) in enumerate(branches):
        gate = jax.nn.sigmoid(_dot(xn, wg_ref[:, n * d:(n + 1) * d]))
        merged = merged + gate * _dot(val.astype(BF16), w_ref[...])
    o_ref[...] = h + _rms(_dot(merged.astype(BF16), wo_ref[...]), gpost_ref[...])


def _merge(h, od, of, zb, z0, z1, z2, g_pre, g_post, conv_w, wg, wbd, wbc, wbf, wo, l, *, tm):
    n, d = h.shape
    tok = lambda w: pl.BlockSpec((tm, w), lambda i: (i, 0))
    lay = lambda a, b: pl.BlockSpec((None, a, b), lambda i: (l, 0, 0))
    return pl.pallas_call(
        _merge_kernel,
        grid=(n // tm,),
        in_specs=[tok(d), tok(Q_WIDTH), tok(Q_WIDTH), tok(CONV_WIDTH), tok(CONV_WIDTH), tok(CONV_WIDTH),
                  tok(CONV_WIDTH), lay(1, d), lay(1, d), lay(CONV_K, CONV_WIDTH),
                  lay(d, N_BRANCHES * d), lay(Q_WIDTH, d), lay(CONV_WIDTH, d), lay(Q_WIDTH, d), lay(d, d)],
        out_specs=tok(d),
        out_shape=jax.ShapeDtypeStruct((n, d), F32),
        compiler_params=_params("parallel"),
        name="merge",
    )(h, od, of, zb, z0, z1, z2, g_pre, g_post, conv_w, wg, wbd, wbc, wbf, wo)


def _rope_tables(pos):
    half = ROPE_DIM // 2
    inv = jnp.power(jnp.float32(ROPE_THETA), -jnp.arange(half, dtype=jnp.float32) * (2.0 / ROPE_DIM))
    ang = pos.astype(jnp.float32)[:, None] * inv[None, :]
    cos, sin = jnp.cos(ang), jnp.sin(ang)
    n = pos.shape[0]
    ones = jnp.ones((n, HEAD_DIM - ROPE_DIM), F32)
    zeros_r = jnp.zeros((n, HEAD_DIM - ROPE_DIM), F32)
    zeros_h = jnp.zeros((n, half), F32)
    cos_t = jnp.concatenate([cos, cos, ones], axis=1)
    sa_t = jnp.concatenate([-sin, zeros_h, zeros_r], axis=1)
    sb_t = jnp.concatenate([zeros_h, sin, zeros_r], axis=1)
    rep = LANES // HEAD_DIM
    return tuple(jnp.tile(x, (1, rep)) for x in (cos_t, sa_t, sb_t))


def _pick(n, prefs):
    for p in prefs:
        if n % p == 0:
            return p
    return n


def kernel(x_prompt, x_sample, cache_dsa_k, cache_dsa_v, cache_idx_k, cache_fox_k, cache_fox_v, cache_fox_logf,
           state_conv, page_table, g_ffn1_pre, g_ffn1_post, w_ffn1_gu, w_ffn1_dn, g_mix_pre, g_mix_post, w_in,
           b_forget, conv_w, w_br_dsa, w_br_conv, w_br_fox, w_out, g_ffn2_pre, g_ffn2_post, w_ffn2_gu, w_ffn2_dn):
    bsz, seq, d = x_prompt.shape
    r, ts, _ = x_sample.shape
    depth = w_in.shape[0]
    n_pool = cache_dsa_k.shape[1]
    n_pages = page_table.shape[1]
    past = n_pages * PAGE_SIZE
    n_p, n_s = bsz * seq, r * ts
    dff = w_ffn1_dn.shape[1]
    assert ts == SUBLANES and seq % LANES == 0

    tm = _pick(math.gcd(n_p, n_s), (512, 256, 128, 64, 32, 16, 8))
    tf = _pick(dff, (1408, 1024, 512, 256, 128))
    tq = _pick(seq, (128,))
    tk = _pick(seq, (512, 256, 128))
    nk = seq // tk

    sizes = (Q_WIDTH, KV_WIDTH, KV_WIDTH, IDX_HEADS * IDX_DIM, IDX_DIM, IDX_HEADS, CONV_WIDTH, CONV_WIDTH,
             CONV_WIDTH, Q_WIDTH, KV_WIDTH, KV_WIDTH, FOX_HEADS, N_BRANCHES * d)
    offs = [0]
    for s_ in sizes:
        offs.append(offs[-1] + s_)
    col = lambda i: w_in[:, :, offs[i]:offs[i + 1]]
    (c_qd, c_kd, c_vd, c_qi, c_ki, c_wi, c_zb, c_zc, c_zx, c_qf, c_kf, c_vf, c_fl, c_g) = [col(i) for i in range(14)]
    ki_pad = jnp.concatenate([c_ki, jnp.zeros((depth, d, LANES - IDX_DIM), F32)], axis=2)
    w_rope = jnp.concatenate([c_qd, c_kd, c_qi, ki_pad], axis=2).astype(BF16)
    w_plain = jnp.concatenate([c_vd, c_zb, c_qf, c_kf, c_vf], axis=2).astype(BF16)
    w_z = jnp.concatenate([c_zc, c_zx], axis=2).astype(BF16)
    w_wi, w_fl, w_g = c_wi.astype(BF16), c_fl.astype(BF16), c_g.astype(BF16)
    w1gu, w1dn = w_ffn1_gu.astype(BF16), w_ffn1_dn.astype(BF16)
    w2gu, w2dn = w_ffn2_gu.astype(BF16), w_ffn2_dn.astype(BF16)
    wbd, wbc, wbf, wo = (w.astype(BF16) for w in (w_br_dsa, w_br_conv, w_br_fox, w_out))
    row = lambda g: g.reshape(depth, 1, -1)
    g1pre, g1post, gmpre, gmpost, g2pre, g2post, bfg = (
        row(g) for g in (g_ffn1_pre, g_ffn1_post, g_mix_pre, g_mix_post, g_ffn2_pre, g_ffn2_post, b_forget))

    pos = jnp.concatenate([jnp.tile(jnp.arange(seq, dtype=I32), bsz),
                           jnp.tile(past + jnp.arange(ts, dtype=I32), r)])
    cos_t, sa_t, sb_t = _rope_tables(pos)

    page_t = lambda c: jnp.moveaxis(c, 2, -1).reshape(depth, n_pool, -1, PAGE_SIZE)
    ckt_d, cvt_d, ckt_f, cvt_f = (page_t(c) for c in (cache_dsa_k, cache_dsa_v, cache_fox_k, cache_fox_v))
    cit, clft = page_t(cache_idx_k), page_t(cache_fox_logf)

    def chunks_t(a, w):
        return a[:n_p].reshape(bsz, nk, tk, w).transpose(0, 1, 3, 2)

    x = jnp.concatenate([x_prompt.reshape(n_p, d), x_sample.reshape(n_s, d)], axis=0)
    rows_p = [[] for _ in range(7)]
    rows_s = [[] for _ in range(7)]
    for l in range(depth):
        h = _ffn(x, g1pre, g1post, w1gu, w1dn, l, tm=tm, tf=tf)
        qd, kd, qi, ki, vd, zb, qf, kf, vf, z, wi, lf = _inproj(
            h, gmpre, cos_t, sa_t, sb_t, bfg, w_rope, w_plain, w_z, w_wi, w_fl, l, tm=tm)

        pr = lambda a: a[:n_p].reshape(bsz, seq, a.shape[1])
        sm = lambda a: a[n_p:].reshape(r, ts, a.shape[1])

        o_dsa = _dsa_prompt(qd, qi, wi, chunks_t(kd, KV_WIDTH), vd, chunks_t(ki, IDX_DIM), bsz, seq, tq=tq, tk=tk)
        o_dsa = _dsa_sample(page_table, ckt_d, cvt_d, cit, qd, qi, wi, kd, vd, ki, o_dsa, n_p, l)

        lf_p = pr(lf)
        lf_blocks = lf_p.reshape(bsz, seq // LANES, LANES, FOX_HEADS).transpose(0, 1, 3, 2)
        f_k = _cumsum_prompt(lf_blocks.reshape(bsz, seq // LANES * FOX_HEADS, LANES))
        f_q = jnp.swapaxes(f_k, 1, 2)
        f_kc = f_k.reshape(bsz, FOX_HEADS, nk, tk).transpose(0, 2, 1, 3)
        o_fox = _fox_prompt(qf, chunks_t(kf, KV_WIDTH), vf, f_q, f_kc, bsz, seq, tq=tq, tk=tk)
        lf_new_t = jnp.pad(jnp.swapaxes(sm(lf), 1, 2), ((0, 0), (0, 0), (0, LANES - ts)))
        o_fox = _fox_sample(page_table, ckt_f, cvt_f, clft, qf, kf, vf, lf_new_t, o_fox, n_p, l)

        z_p, z_s = pr(z), sm(z)
        zp_p = jnp.concatenate([jnp.zeros((bsz, CONV_K - 1, CONV_WIDTH), F32), z_p], axis=1)
        zp_s = jnp.concatenate([state_conv[l].astype(F32), z_s], axis=1)
        taps = [jnp.concatenate([zp_p[:, j:j + seq].reshape(n_p, CONV_WIDTH),
                                 zp_s[:, j:j + ts].reshape(n_s, CONV_WIDTH)], axis=0) for j in range(CONV_K)]

        h = _merge(h, o_dsa, o_fox, zb, taps[0], taps[1], taps[2], gmpre, gmpost, conv_w,
                   w_g, wbd, wbc, wbf, wo, l, tm=tm)
        x = _ffn(h, g2pre, g2post, w2gu, w2dn, l, tm=tm, tf=tf)

        heads = lambda a: a.reshape(*a.shape[:-1], DSA_KV_HEADS, HEAD_DIM)
        for dst, take, zp, length in ((rows_p, pr, zp_p, seq), (rows_s, sm, zp_s, ts)):
            dst[0].append(heads(take(kd)))
            dst[1].append(heads(take(vd)))
            dst[2].append(take(ki))
            dst[3].append(heads(take(kf)))
            dst[4].append(heads(take(vf)))
            dst[5].append(take(lf))
            dst[6].append(zp[:, length:length + CONV_K - 1])

    sp = [jnp.stack(a, axis=0) for a in rows_p]
    ss = [jnp.stack(a, axis=0) for a in rows_s]
    return (x[:n_p].reshape(bsz, seq, d), x[n_p:].reshape(r, ts, d),
            sp[0], sp[1], sp[2], sp[3], sp[4], sp[5], sp[6],
            ss[0], ss[1], ss[2], ss[3], ss[4], ss[5], ss[6])
```

```python
import functools
import math

import jax
import jax.numpy as jnp
from jax import lax
from jax.experimental import pallas as pl
from jax.experimental.pallas import tpu as pltpu

HEAD_DIM = 64
DSA_HEADS = 8
DSA_KV_HEADS = 4
IDX_HEADS = 8
IDX_DIM = 64
TOPK_MAX = 256
CONV_WIDTH = 512
CONV_K = 3
FOX_HEADS = 8
FOX_KV_HEADS = 4
ROPE_THETA = 500000.0
ROPE_DIM = HEAD_DIM // 4
NORM_EPS = 1e-6
N_BRANCHES = 3
PAGE_SIZE = 128

LANES = 128
SUBLANES = 8
KV_WIDTH = DSA_KV_HEADS * HEAD_DIM
Q_WIDTH = DSA_HEADS * HEAD_DIM
VMEM_LIMIT = 56 * 1024 * 1024

F32 = jnp.float32
BF16 = jnp.bfloat16
I32 = jnp.int32
NEG = -1e30
INT_MIN = -(2 ** 31)
NEG_INF_KEY = INT_MIN + 0x7FFFFF
THR_ROW_GROUPS = 2

assert DSA_HEADS == FOX_HEADS and DSA_KV_HEADS == FOX_KV_HEADS
assert DSA_HEADS // DSA_KV_HEADS == 2 and 2 * HEAD_DIM == LANES


def _rms(x, g):
    return x * lax.rsqrt(jnp.mean(x * x, axis=-1, keepdims=True) + NORM_EPS) * g


def _nt_dot(a, b):
    return lax.dot_general(a, b, (((1,), (1,)), ((), ())), preferred_element_type=F32)


def _dot(a, b):
    return jnp.dot(a, b, preferred_element_type=F32)


def _params(*sem):
    return pltpu.CompilerParams(dimension_semantics=sem, vmem_limit_bytes=VMEM_LIMIT)


def _ffn_kernel(x_ref, gpre_ref, gpost_ref, wa_ref, wb_ref, wd_ref, o_ref, xn_ref, acc_ref, *, nf):
    j = pl.program_id(1)

    @pl.when(j == 0)
    def _():
        xn_ref[...] = _rms(x_ref[...], gpre_ref[...]).astype(BF16)
        acc_ref[...] = jnp.zeros_like(acc_ref)

    xn = xn_ref[...]
    a = _dot(xn, wa_ref[...])
    b = _dot(xn, wb_ref[...])
    act = (a * jax.nn.sigmoid(a) * b).astype(BF16)
    acc_ref[...] += _dot(act, wd_ref[...])

    @pl.when(j == nf - 1)
    def _():
        o_ref[...] = x_ref[...] + 0.5 * _rms(acc_ref[...], gpost_ref[...])


def _ffn(x, g_pre, g_post, w_gu, w_dn, l, *, tm, tf):
    n, d = x.shape
    dff = w_dn.shape[1]
    nf = dff // tf
    return pl.pallas_call(
        functools.partial(_ffn_kernel, nf=nf),
        grid=(n // tm, nf),
        in_specs=[
            pl.BlockSpec((tm, d), lambda i, j: (i, 0)),
            pl.BlockSpec((None, 1, d), lambda i, j: (l, 0, 0)),
            pl.BlockSpec((None, 1, d), lambda i, j: (l, 0, 0)),
            pl.BlockSpec((None, d, tf), lambda i, j: (l, 0, j)),
            pl.BlockSpec((None, d, tf), lambda i, j: (l, 0, j + nf)),
            pl.BlockSpec((None, tf, d), lambda i, j: (l, j, 0)),
        ],
        out_specs=pl.BlockSpec((tm, d), lambda i, j: (i, 0)),
        out_shape=jax.ShapeDtypeStruct((n, d), F32),
        scratch_shapes=[pltpu.VMEM((tm, d), BF16), pltpu.VMEM((tm, d), F32)],
        compiler_params=_params("parallel", "arbitrary"),
        name="ffn",
    )(x, g_pre, g_post, w_gu, w_gu, w_dn)


ROPE_COLS = Q_WIDTH + KV_WIDTH + Q_WIDTH + LANES
PLAIN_COLS = KV_WIDTH + CONV_WIDTH + Q_WIDTH + 2 * KV_WIDTH


def _log_sigmoid(x):
    return jnp.minimum(x, 0.0) - jnp.log1p(jnp.exp(-jnp.abs(x)))


def _inproj_kernel(x_ref, g_ref, cos_ref, sa_ref, sb_ref, bf_ref, wr_ref, wp_ref, wz_ref, wwi_ref, wfl_ref,
                   qd_o, kd_o, qi_o, ki_o, vd_o, zb_o, qf_o, kf_o, vf_o, z_o, wi_o, lf_o):
    xn = _rms(x_ref[...], g_ref[...]).astype(BF16)
    cos, sa, sb = cos_ref[...], sa_ref[...], sb_ref[...]

    r = _dot(xn, wr_ref[...])
    outs = []
    for c in range(ROPE_COLS // LANES):
        v = r[:, c * LANES:(c + 1) * LANES]
        up = pltpu.roll(v, LANES - ROPE_DIM // 2, axis=1)
        dn = pltpu.roll(v, ROPE_DIM // 2, axis=1)
        outs.append(v * cos + up * sa + dn * sb)
    nq = Q_WIDTH // LANES
    nk = KV_WIDTH // LANES
    qd_o[...] = jnp.concatenate(outs[:nq], axis=1)
    kd_o[...] = jnp.concatenate(outs[nq:nq + nk], axis=1)
    qi_o[...] = jnp.concatenate(outs[nq + nk:2 * nq + nk], axis=1)
    ki_o[...] = outs[2 * nq + nk][:, :IDX_DIM]

    p = _dot(xn, wp_ref[...])
    o = 0
    for ref, w in ((vd_o, KV_WIDTH), (zb_o, CONV_WIDTH), (qf_o, Q_WIDTH), (kf_o, KV_WIDTH), (vf_o, KV_WIDTH)):
        ref[...] = p[:, o:o + w]
        o += w

    zz = _dot(xn, wz_ref[...])
    z_o[...] = zz[:, :CONV_WIDTH] * zz[:, CONV_WIDTH:]

    wi_o[...] = _dot(xn, wwi_ref[...]) * (IDX_HEADS ** -0.5)
    lf_o[...] = _log_sigmoid(_dot(xn, wfl_ref[...]) + bf_ref[...])


def _inproj(x, g, cos, sa, sb, b_forget, wr, wp, wz, wwi, wfl, l, *, tm):
    n, d = x.shape
    tok = lambda w: pl.BlockSpec((tm, w), lambda i: (i, 0))
    lay = lambda a, b: pl.BlockSpec((None, a, b), lambda i: (l, 0, 0))
    widths = (Q_WIDTH, KV_WIDTH, Q_WIDTH, IDX_DIM, KV_WIDTH, CONV_WIDTH, Q_WIDTH, KV_WIDTH, KV_WIDTH,
              CONV_WIDTH, IDX_HEADS, FOX_HEADS)
    return pl.pallas_call(
        _inproj_kernel,
        grid=(n // tm,),
        in_specs=[tok(d), lay(1, d), tok(LANES), tok(LANES), tok(LANES), lay(1, FOX_HEADS),
                  lay(d, ROPE_COLS), lay(d, PLAIN_COLS), lay(d, 2 * CONV_WIDTH),
                  lay(d, IDX_HEADS), lay(d, FOX_HEADS)],
        out_specs=[tok(w) for w in widths],
        out_shape=[jax.ShapeDtypeStruct((n, w), F32) for w in widths],
        compiler_params=_params("parallel"),
        name="inproj",
    )(x, g, cos, sa, sb, b_forget, wr, wp, wz, wwi, wfl)


def _block_cumsum(x, n_blocks):
    rows = n_blocks * SUBLANES
    padded = -(-rows // LANES) * LANES
    if padded != rows:
        x = jnp.concatenate([x, jnp.zeros((padded - rows, LANES), F32)], axis=0)
    hi = functools.partial(jnp.dot, preferred_element_type=F32, precision=lax.Precision.HIGHEST)
    r0 = lax.broadcasted_iota(I32, (LANES, LANES), 0)
    c0 = lax.broadcasted_iota(I32, (LANES, LANES), 1)
    within = hi(x, jnp.where(r0 <= c0, 1.0, 0.0).astype(F32))
    totals = jnp.broadcast_to(within[:, LANES - 1:LANES], (padded, LANES))
    dist = lax.broadcasted_iota(I32, (padded, padded), 0) - lax.broadcasted_iota(I32, (padded, padded), 1)
    earlier = jnp.where((dist > 0) & ((dist & (SUBLANES - 1)) == 0), 1.0, 0.0).astype(F32)
    full = within + hi(earlier, totals)
    return [full[b * SUBLANES:(b + 1) * SUBLANES] for b in range(n_blocks)]


def _cumsum_kernel(x_ref, o_ref, *, n_blocks):
    o_ref[...] = jnp.concatenate(_block_cumsum(x_ref[...], n_blocks), axis=1)


def _cumsum_prompt(lf_blocks):
    b, rows, _ = lf_blocks.shape
    n_blocks = rows // FOX_HEADS
    t = n_blocks * LANES
    return pl.pallas_call(
        functools.partial(_cumsum_kernel, n_blocks=n_blocks),
        grid=(b,),
        in_specs=[pl.BlockSpec((None, rows, LANES), lambda i: (i, 0, 0))],
        out_specs=pl.BlockSpec((None, FOX_HEADS, t), lambda i: (i, 0, 0)),
        out_shape=jax.ShapeDtypeStruct((b, FOX_HEADS, t), F32),
        compiler_params=_params("parallel"),
        name="cumsum_prompt",
    )(lf_blocks)


def _q_block_diag(q):
    t = q.shape[0]
    lane = lax.broadcasted_iota(I32, (t, LANES), 1)
    zeros = jnp.zeros((t, LANES), F32)
    blocks = []
    for h in range(DSA_HEADS):
        g, r = divmod(h, 2)
        src = q[:, g * LANES:(g + 1) * LANES]
        if r != g % 2:
            src = pltpu.roll(src, HEAD_DIM, axis=1)
        keep = (lane < HEAD_DIM) if g % 2 == 0 else (lane >= HEAD_DIM)
        m = jnp.where(keep, src, 0.0)
        blocks.append(jnp.concatenate([m, zeros] if g // 2 == 0 else [zeros, m], axis=1))
    return jnp.concatenate(blocks, axis=0).astype(BF16)


def _head_rows(q):
    return jnp.concatenate([q[:, h * IDX_DIM:(h + 1) * IDX_DIM] for h in range(IDX_HEADS)], axis=0)


def _extract_heads(acc, t):
    outs = []
    for h in range(DSA_HEADS):
        g = h // 2
        outs.append(acc[h * t:(h + 1) * t, g * HEAD_DIM:(g + 1) * HEAD_DIM])
    return jnp.concatenate(outs, axis=1)


def _sort_key(score):
    bits = pltpu.bitcast(score, I32)
    return jnp.where(bits < 0, bits ^ 0x7FFFFFFF, bits)


def _indexer_score(s, wi, t):
    score = jnp.zeros((t, s.shape[1]), F32)
    for h in range(s.shape[0] // t):
        score = score + jnp.maximum(s[h * t:(h + 1) * t], 0.0) * wi[:, h:h + 1]
    return jnp.where(score == 0.0, 0.0, score)


def _fold_lanes(x):
    out = x[:, :LANES]
    for j in range(1, x.shape[1] // LANES):
        out = out + x[:, j * LANES:(j + 1) * LANES]
    return out


LOG2E = 1.4426950408889634


def _online_softmax_step(qbd, kt, v, bias_fn, m_ref, l_ref, acc_ref, tq):
    tk = kt.shape[1]
    group = DSA_HEADS // DSA_KV_HEADS
    for g in range(DSA_HEADS // group):
        h0, h1 = g * group, (g + 1) * group
        s3 = _dot(qbd[h0 * tq:h1 * tq], kt).reshape(group, tq, tk) + bias_fn(h0, h1)
        m_old = m_ref[h0:h1]
        m_new = jnp.maximum(m_old, jnp.max(s3, axis=2, keepdims=True))
        p = jnp.exp2(s3 - m_new)
        alpha = jnp.exp2(m_old - m_new)
        l_ref[h0:h1] = alpha * l_ref[h0:h1] + jnp.sum(p, axis=2, keepdims=True)
        m_ref[h0:h1] = m_new
        pv = _dot(p.reshape(group * tq, tk).astype(BF16), v)
        acc_ref[h0:h1] = alpha * acc_ref[h0:h1] + pv.reshape(group, tq, KV_WIDTH)


def _softmax_scratch(tq):
    return [pltpu.VMEM((DSA_HEADS, tq, 1), F32), pltpu.VMEM((DSA_HEADS, tq, 1), F32),
            pltpu.VMEM((DSA_HEADS, tq, KV_WIDTH), F32)]


def _softmax_result(l_ref, acc_ref, tq):
    return _extract_heads((acc_ref[...] / l_ref[...]).reshape(DSA_HEADS * tq, KV_WIDTH), tq)


def _softmax_init(m_ref, l_ref, acc_ref):
    m_ref[...] = jnp.full_like(m_ref, NEG)
    l_ref[...] = jnp.zeros_like(l_ref)
    acc_ref[...] = jnp.zeros_like(acc_ref)


def _fox_prompt_kernel(q_ref, kt_ref, v_ref, fq_ref, fk_ref, o_ref, m_ref, l_ref, acc_ref, *, tq, tk):
    i = pl.program_id(1)
    qbd = _q_block_diag(q_ref[...] * (HEAD_DIM ** -0.5 * LOG2E))
    fq = fq_ref[...] * LOG2E
    fq_rep = jnp.stack([jnp.broadcast_to(fq[:, h:h + 1], (tq, LANES)) for h in range(FOX_HEADS)], axis=0)
    _softmax_init(m_ref, l_ref, acc_ref)
    qpos = i * tq + lax.broadcasted_iota(I32, (tq, tk), 0)
    kiota = lax.broadcasted_iota(I32, (tq, tk), 1)

    def chunk(c, carry, *, masked):
        start = pl.multiple_of(c * tk, tk)
        v = v_ref[pl.ds(start, tk), :].astype(BF16)
        fk = fk_ref[c] * LOG2E
        fk3 = jnp.stack([fk[h:h + 1, :] for h in range(FOX_HEADS)], axis=0)
        causal = (kiota + c * tk) <= qpos

        def bias(h0, h1):
            b = jnp.concatenate([fq_rep[h0:h1]] * (tk // LANES), axis=2) - fk3[h0:h1]
            return jnp.where(causal[None], b, NEG) if masked else b

        _online_softmax_step(qbd, kt_ref[c].astype(BF16), v, bias, m_ref, l_ref, acc_ref, tq)
        return carry

    n_full = (i * tq + 1) // tk
    lax.fori_loop(0, n_full, functools.partial(chunk, masked=False), 0)
    lax.fori_loop(n_full, (i * tq + tq - 1) // tk + 1, functools.partial(chunk, masked=True), 0)
    o_ref[...] = _softmax_result(l_ref, acc_ref, tq)


def _fox_prompt(q, kt, v, fq, fk, bsz, seq, *, tq, tk):
    n = q.shape[0]
    nq, nk = seq // tq, seq // tk
    return pl.pallas_call(
        functools.partial(_fox_prompt_kernel, tq=tq, tk=tk),
        grid=(bsz, nq),
        in_specs=[
            pl.BlockSpec((tq, Q_WIDTH), lambda bi, i: (bi * nq + i, 0)),
            pl.BlockSpec((None, nk, KV_WIDTH, tk), lambda bi, i: (bi, 0, 0, 0)),
            pl.BlockSpec((seq, KV_WIDTH), lambda bi, i: (bi, 0)),
            pl.BlockSpec((None, tq, FOX_HEADS), lambda bi, i: (bi, i, 0)),
            pl.BlockSpec((None, nk, FOX_HEADS, tk), lambda bi, i: (bi, 0, 0, 0)),
        ],
        out_specs=pl.BlockSpec((tq, Q_WIDTH), lambda bi, i: (bi * nq + i, 0)),
        out_shape=jax.ShapeDtypeStruct((n, Q_WIDTH), F32),
        scratch_shapes=_softmax_scratch(tq),
        compiler_params=_params("parallel", "parallel"),
        name="fox_prompt",
    )(q, kt, v, fq, fk)


def _dsa_prompt_kernel(q_ref, qit_ref, wit_ref, kt_ref, v_ref, ki_ref, o_ref,
                       key_ref, thr_ref, j_ref, m_ref, l_ref, acc_ref, *, tq, tk, top_k, seq):
    i = pl.program_id(1)
    n_vis = (i * tq + tq - 1) // tk + 1
    kpos = lax.broadcasted_iota(I32, (tk, tq), 0)
    qpos = i * tq + lax.broadcasted_iota(I32, (tk, tq), 1)
    kf = float(top_k)

    qit = qit_ref[...].astype(BF16)
    qi_t = jnp.concatenate([qit[h * IDX_DIM:(h + 1) * IDX_DIM] for h in range(IDX_HEADS)], axis=1)
    wit = wit_ref[...] * (IDX_DIM ** -0.5)

    def score_chunk(c, carry):
        start = pl.multiple_of(c * tk, tk)
        s = _dot(ki_ref[pl.ds(start, tk), :].astype(BF16), qi_t)
        score = jnp.zeros((tk, tq), F32)
        for h in range(IDX_HEADS):
            score = score + jnp.maximum(s[:, h * tq:(h + 1) * tq], 0.0) * wit[h:h + 1, :]
        score = jnp.where(score == 0.0, 0.0, score)
        score = jnp.where((kpos + c * tk) <= qpos, score, -jnp.inf)
        key_ref[c] = _sort_key(score)
        return carry

    lax.fori_loop(0, n_vis, score_chunk, 0)

    def partial_count(mask):
        ones = jnp.where(mask, 1.0, 0.0).reshape(8, tk // (8 * SUBLANES), SUBLANES, tq)
        c = [jnp.sum(ones[g], axis=0) for g in range(8)]
        return ((c[0] + c[1]) + (c[2] + c[3])) + ((c[4] + c[5]) + (c[6] + c[7]))

    def count(*preds):
        def body(c, cnts):
            key, pos = key_ref[c], kpos + c * tk
            return tuple(cnt + partial_count(pred(key, pos)) for cnt, pred in zip(cnts, preds))
        zero = jnp.zeros((SUBLANES, tq), F32)
        totals = lax.fori_loop(0, n_vis, body, tuple(zero for _ in preds))
        return tuple(jnp.sum(x, axis=0, keepdims=True) for x in totals)

    def search(n_chunks):
        def step(b, cand):
            trial = cand ^ lax.shift_left(jnp.int32(1), 31 - b)
            cnt = jnp.zeros((SUBLANES, tq), F32)
            for c in range(n_chunks):
                cnt = cnt + partial_count(key_ref[c] >= trial)
            return jnp.where(jnp.sum(cnt, axis=0, keepdims=True) >= kf, trial, cand)

        return lax.fori_loop(0, 32, step, jnp.full((1, tq), INT_MIN, I32))

    for n_chunks in range(1, seq // tk + 1):
        @pl.when(n_vis == n_chunks)
        def _():
            thr_ref[...] = search(n_chunks)

    thr = thr_ref[...]

    n_gt, n_ge = count(lambda key, pos: key > thr, lambda key, pos: key >= thr)
    need = kf - n_gt
    j_ref[...] = jnp.full_like(j_ref, seq)
    tied = jnp.where((n_ge > kf) & (thr != NEG_INF_KEY), 1.0, 0.0)

    @pl.when(jnp.max(tied) > 0.0)
    def _():
        def idx_step(b, lo):
            trial = lo | lax.shift_left(jnp.int32(1), int(math.log2(seq)) - 1 - b)
            (c,) = count(lambda key, pos: jnp.where(key == thr, pos, seq) < trial)
            return jnp.where(c < need, trial, lo)

        j_ref[...] = lax.fori_loop(0, int(math.log2(seq)), idx_step, jnp.zeros((1, tq), I32))

    jmax = j_ref[...]

    qbd = _q_block_diag(q_ref[...] * (HEAD_DIM ** -0.5 * LOG2E))
    _softmax_init(m_ref, l_ref, acc_ref)

    def attn_chunk(c, carry):
        start = pl.multiple_of(c * tk, tk)
        v = v_ref[pl.ds(start, tk), :].astype(BF16)
        key = key_ref[c]
        tie_bias = jnp.where(jnp.where(key == thr, kpos + c * tk, seq + 1) <= jmax, 0.0, NEG)
        bias = jnp.where(key > NEG_INF_KEY, jnp.where(key > thr, 0.0, tie_bias), NEG).T
        _online_softmax_step(qbd, kt_ref[c].astype(BF16), v, lambda h0, h1: bias[None], m_ref, l_ref, acc_ref, tq)
        return carry

    lax.fori_loop(0, n_vis, attn_chunk, 0)
    o_ref[...] = _softmax_result(l_ref, acc_ref, tq)


def _dsa_prompt(q, qit, wit, kt, v, ki, bsz, seq, *, tq, tk):
    n = q.shape[0]
    nq, nk = seq // tq, seq // tk
    top_k = min(TOPK_MAX, seq // 4)
    return pl.pallas_call(
        functools.partial(_dsa_prompt_kernel, tq=tq, tk=tk, top_k=top_k, seq=seq),
        grid=(bsz, nq),
        in_specs=[
            pl.BlockSpec((tq, Q_WIDTH), lambda bi, i: (bi * nq + i, 0)),
            pl.BlockSpec((Q_WIDTH, tq), lambda bi, i: (0, bi * nq + i)),
            pl.BlockSpec((IDX_HEADS, tq), lambda bi, i: (0, bi * nq + i)),
            pl.BlockSpec((None, nk, KV_WIDTH, tk), lambda bi, i: (bi, 0, 0, 0)),
            pl.BlockSpec((seq, KV_WIDTH), lambda bi, i: (bi, 0)),
            pl.BlockSpec((seq, IDX_DIM), lambda bi, i: (bi, 0)),
        ],
        out_specs=pl.BlockSpec((tq, Q_WIDTH), lambda bi, i: (bi * nq + i, 0)),
        out_shape=jax.ShapeDtypeStruct((n, Q_WIDTH), F32),
        scratch_shapes=[pltpu.VMEM((nk, tk, tq), I32), pltpu.VMEM((1, tq), I32), pltpu.VMEM((1, tq), I32)]
        + _softmax_scratch(tq),
        compiler_params=_params("parallel", "parallel"),
        name="dsa_prompt",
    )(q, qit, wit, kt, v, ki)


def _page_specs(n_pages, rows, l):
    return [pl.BlockSpec((None, None, rows, PAGE_SIZE),
                         functools.partial(lambda b, pt, p: (l, pt[b * n_pages + p], 0, 0), p=p))
            for p in range(n_pages)]


def _pad_rows(x, rows):
    return jnp.concatenate([x, jnp.zeros((rows - x.shape[0], x.shape[1]), x.dtype)], axis=0)


def _paged_logits(lhs, pages_t, new_rows):
    blocks = [_dot(lhs, p[...].astype(BF16)) for p in pages_t]
    blocks.append(_nt_dot(lhs, _pad_rows(new_rows, PAGE_SIZE).astype(BF16)))
    return jnp.concatenate(blocks, axis=1)


def _softmax_pv(logits, v_pages_t, v_new, t):
    m = jnp.max(logits, axis=1, keepdims=True)
    p = jnp.exp(logits - m)
    l = jnp.sum(p, axis=1, keepdims=True)
    pb = p.astype(BF16)
    acc = jnp.zeros((logits.shape[0], KV_WIDTH), F32)
    for n, vp in enumerate(v_pages_t):
        acc = acc + _nt_dot(pb[:, n * PAGE_SIZE:(n + 1) * PAGE_SIZE], vp[...].astype(BF16))
    n = len(v_pages_t)
    acc = acc + _dot(pb[:, n * PAGE_SIZE:(n + 1) * PAGE_SIZE], _pad_rows(v_new, PAGE_SIZE).astype(BF16))
    return _extract_heads(acc / l, t)


def _visible(t, length, past):
    pos = lax.broadcasted_iota(I32, (t, length), 1)
    return pos, pos <= past + lax.broadcasted_iota(I32, (t, length), 0)


def _fox_sample_body(kp, vp, lp, q_ref, kn_ref, vn_ref, lfn_ref, t):
    n_pages = len(kp)
    past = n_pages * PAGE_SIZE
    length = past + PAGE_SIZE

    f_tiles = _block_cumsum(jnp.concatenate([p[...] for p in lp] + [lfn_ref[...]], axis=0), n_pages + 1)
    fk = jnp.concatenate(f_tiles, axis=1)
    lane = lax.broadcasted_iota(I32, (t, LANES), 1)
    diag = lane == lax.broadcasted_iota(I32, (t, LANES), 0)

    qbd = _q_block_diag(q_ref[...] * (HEAD_DIM ** -0.5))
    s = _paged_logits(qbd, kp, kn_ref[...])
    _, visible = _visible(t, length, past)
    rows = []
    for h in range(FOX_HEADS):
        f_new = jnp.broadcast_to(f_tiles[n_pages][h:h + 1, :], (t, LANES))
        fq = jnp.sum(jnp.where(diag, f_new, 0.0), axis=1, keepdims=True)
        sh = s[h * t:(h + 1) * t] + (fq - fk[h:h + 1, :])
        rows.append(jnp.where(visible, sh, NEG))
    return _softmax_pv(jnp.concatenate(rows, axis=0), vp, vn_ref[...], t)


def _sample_topk(key, pos, top_k, length, j_ref):
    t = key.shape[0]
    kf = float(top_k)
    count = lambda mask: jnp.sum(jnp.where(mask, 1.0, 0.0), axis=1, keepdims=True)

    def digit_pass(cand, shift, n_trials):
        digit = jnp.zeros((t, 1), I32)
        for j in range(1, n_trials + 1):
            trial = cand + jnp.left_shift(jnp.int32(j), shift)
            digit = digit + jnp.where(count(key >= trial) >= kf, 1, 0)
        return cand + jnp.left_shift(digit, shift)

    thr = digit_pass(jnp.full((t, 1), INT_MIN, I32), 30, 3)
    for shift in range(27, -1, -3):
        thr = digit_pass(thr, shift, 7)

    need = kf - count(key > thr)
    n_ge = count(key >= thr)
    j_ref[...] = jnp.full_like(j_ref, length)
    tied = jnp.where((n_ge > kf) & (thr != NEG_INF_KEY), 1.0, 0.0)

    @pl.when(jnp.max(tied) > 0.0)
    def _():
        n_bits = int(math.ceil(math.log2(length)))
        tie_pos = jnp.where(key == thr, pos, length)

        def idx_step(b, lo):
            trial = lo | lax.shift_left(jnp.int32(1), n_bits - 1 - b)
            return jnp.where(count(tie_pos < trial) < need, trial, lo)

        j_ref[...] = lax.fori_loop(0, n_bits, idx_step, jnp.zeros((t, 1), I32))

    return thr, j_ref[...]


def _dsa_sample_body(kp, vp, ip, q_ref, qi_ref, wi_ref, kn_ref, vn_ref, in_ref, j_ref, t, top_k):
    n_pages = len(kp)
    past = n_pages * PAGE_SIZE
    length = past + PAGE_SIZE

    qi = _head_rows(qi_ref[...]).astype(BF16)
    score = _indexer_score(_paged_logits(qi, ip, in_ref[...]), wi_ref[...] * (IDX_DIM ** -0.5), t)
    pos, visible = _visible(t, length, past)
    key = _sort_key(jnp.where(visible, score, -jnp.inf))
    thr, jmax = _sample_topk(key, pos, top_k, length, j_ref)
    tie_bias = jnp.where(jnp.where(key == thr, pos, length + 1) <= jmax, 0.0, NEG)
    bias = jnp.where(key > NEG_INF_KEY, jnp.where(key > thr, 0.0, tie_bias), NEG)

    qbd = _q_block_diag(q_ref[...] * (HEAD_DIM ** -0.5))
    logits = _paged_logits(qbd, kp, kn_ref[...]) + jnp.concatenate([bias] * DSA_HEADS, axis=0)
    return _softmax_pv(logits, vp, vn_ref[...], t)


def _sample_attn_kernel(pt_ref, *refs, n_pages, t, top_k):
    pages = [refs[n * n_pages:(n + 1) * n_pages] for n in range(6)]
    kd, vd, ki, kf, vf, lf = pages
    (qd_ref, qi_ref, wi_ref, kdn_ref, vdn_ref, kin_ref, qf_ref, kfn_ref, vfn_ref, lfn_ref, _, _,
     od_ref, of_ref, j_ref) = refs[6 * n_pages:]
    of_ref[...] = _fox_sample_body(kf, vf, lf, qf_ref, kfn_ref, vfn_ref, lfn_ref, t)
    od_ref[...] = _dsa_sample_body(kd, vd, ki, qd_ref, qi_ref, wi_ref, kdn_ref, vdn_ref, kin_ref, j_ref, t, top_k)


def _sample_attn(page_table, caches_t, qd, qi, wi, kd, vd, ki, qf, kf, vf, lf_new_t, o_dsa, o_fox, n_p, l):
    r, n_pages = page_table.shape
    t = (qd.shape[0] - n_p) // r
    off = n_p // t
    top_k = min(TOPK_MAX, (n_pages * PAGE_SIZE + t) // 4)
    req = lambda w: pl.BlockSpec((t, w), lambda bi, pt: (off + bi, 0))
    page_in = []
    for c in caches_t:
        page_in += _page_specs(n_pages, c.shape[2], l)
    row_widths = (Q_WIDTH, Q_WIDTH, IDX_HEADS, KV_WIDTH, KV_WIDTH, IDX_DIM, Q_WIDTH, KV_WIDTH, KV_WIDTH)
    grid_spec = pltpu.PrefetchScalarGridSpec(
        num_scalar_prefetch=1,
        grid=(r,),
        in_specs=(page_in + [req(w) for w in row_widths]
                  + [pl.BlockSpec((None, FOX_HEADS, LANES), lambda bi, pt: (bi, 0, 0)),
                     pl.BlockSpec(memory_space=pl.ANY), pl.BlockSpec(memory_space=pl.ANY)]),
        out_specs=[req(Q_WIDTH), req(Q_WIDTH)],
        scratch_shapes=[pltpu.VMEM((t, 1), I32)],
    )
    n_in = 1 + 6 * n_pages + len(row_widths) + 3
    page_args = [c for c in caches_t for _ in range(n_pages)]
    return pl.pallas_call(
        functools.partial(_sample_attn_kernel, n_pages=n_pages, t=t, top_k=top_k),
        grid_spec=grid_spec,
        out_shape=[jax.ShapeDtypeStruct(o_dsa.shape, F32), jax.ShapeDtypeStruct(o_fox.shape, F32)],
        input_output_aliases={n_in - 2: 0, n_in - 1: 1},
        compiler_params=_params("parallel"),
        name="sample_attn",
    )(page_table.reshape(-1), *page_args, qd, qi, wi, kd, vd, ki, qf, kf, vf, lf_new_t, o_dsa, o_fox)


def _merge_kernel(h_ref, od_ref, of_ref, zb_ref, z0_ref, z1_ref, z2_ref, gpre_ref, gpost_ref, cw_ref,
                  wg_ref, wbd_ref, wbc_ref, wbf_ref, wo_ref, o_ref):
    h = h_ref[...]
    d = h.shape[1]
    xn = _rms(h, gpre_ref[...]).astype(BF16)
    cw = cw_ref[...]
    conv = cw[0:1] * z0_ref[...] + cw[1:2] * z1_ref[...] + cw[2:3] * z2_ref[...]
    branches = ((od_ref[...], wbd_ref), (zb_ref[...] * conv, wbc_ref), (of_ref[...], wbf_ref))
    merged = jnp.zeros_like(h)
    for n, (val, w_ref) in enumerate(branches):
        gate = jax.nn.sigmoid(_dot(xn, wg_ref[:, n * d:(n + 1) * d]))
        merged = merged + gate * _dot(val.astype(BF16), w_ref[...])
    o_ref[...] = h + _rms(_dot(merged.astype(BF16), wo_ref[...]), gpost_ref[...])


def _merge(h, od, of, zb, z0, z1, z2, g_pre, g_post, conv_w, wg, wbd, wbc, wbf, wo, l, *, tm):
    n, d = h.shape
    tok = lambda w: pl.BlockSpec((tm, w), lambda i: (i, 0))
    lay = lambda a, b: pl.BlockSpec((None, a, b), lambda i: (l, 0, 0))
    return pl.pallas_call(
        _merge_kernel,
        grid=(n // tm,),
        in_specs=[tok(d), tok(Q_WIDTH), tok(Q_WIDTH), tok(CONV_WIDTH), tok(CONV_WIDTH), tok(CONV_WIDTH),
                  tok(CONV_WIDTH), lay(1, d), lay(1, d), lay(CONV_K, CONV_WIDTH),
                  lay(d, N_BRANCHES * d), lay(Q_WIDTH, d), lay(CONV_WIDTH, d), lay(Q_WIDTH, d), lay(d, d)],
        out_specs=tok(d),
        out_shape=jax.ShapeDtypeStruct((n, d), F32),
        compiler_params=_params("parallel"),
        name="merge",
    )(h, od, of, zb, z0, z1, z2, g_pre, g_post, conv_w, wg, wbd, wbc, wbf, wo)


def _rope_tables(pos):
    half = ROPE_DIM // 2
    inv = jnp.power(jnp.float32(ROPE_THETA), -jnp.arange(half, dtype=jnp.float32) * (2.0 / ROPE_DIM))
    ang = pos.astype(jnp.float32)[:, None] * inv[None, :]
    cos, sin = jnp.cos(ang), jnp.sin(ang)
    n = pos.shape[0]
    ones = jnp.ones((n, HEAD_DIM - ROPE_DIM), F32)
    zeros_r = jnp.zeros((n, HEAD_DIM - ROPE_DIM), F32)
    zeros_h = jnp.zeros((n, half), F32)
    cos_t = jnp.concatenate([cos, cos, ones], axis=1)
    sa_t = jnp.concatenate([-sin, zeros_h, zeros_r], axis=1)
    sb_t = jnp.concatenate([zeros_h, sin, zeros_r], axis=1)
    rep = LANES // HEAD_DIM
    return tuple(jnp.tile(x, (1, rep)) for x in (cos_t, sa_t, sb_t))


def _pick(n, prefs):
    for p in prefs:
        if n % p == 0:
            return p
    return n


def kernel(x_prompt, x_sample, cache_dsa_k, cache_dsa_v, cache_idx_k, cache_fox_k, cache_fox_v, cache_fox_logf,
           state_conv, page_table, g_ffn1_pre, g_ffn1_post, w_ffn1_gu, w_ffn1_dn, g_mix_pre, g_mix_post, w_in,
           b_forget, conv_w, w_br_dsa, w_br_conv, w_br_fox, w_out, g_ffn2_pre, g_ffn2_post, w_ffn2_gu, w_ffn2_dn):
    bsz, seq, d = x_prompt.shape
    r, ts, _ = x_sample.shape
    depth = w_in.shape[0]
    n_pool = cache_dsa_k.shape[1]
    n_pages = page_table.shape[1]
    past = n_pages * PAGE_SIZE
    n_p, n_s = bsz * seq, r * ts
    dff = w_ffn1_dn.shape[1]
    assert ts == SUBLANES and seq % LANES == 0

    tm = _pick(math.gcd(n_p, n_s), (512, 256, 128, 64, 32, 16, 8))
    tf = _pick(dff, (1408, 1024, 512, 256, 128))
    tq = _pick(seq, (128,))
    tk = _pick(seq, (512, 256, 128))
    nk = seq // tk

    sizes = (Q_WIDTH, KV_WIDTH, KV_WIDTH, IDX_HEADS * IDX_DIM, IDX_DIM, IDX_HEADS, CONV_WIDTH, CONV_WIDTH,
             CONV_WIDTH, Q_WIDTH, KV_WIDTH, KV_WIDTH, FOX_HEADS, N_BRANCHES * d)
    offs = [0]
    for s_ in sizes:
        offs.append(offs[-1] + s_)
    col = lambda i: w_in[:, :, offs[i]:offs[i + 1]]
    (c_qd, c_kd, c_vd, c_qi, c_ki, c_wi, c_zb, c_zc, c_zx, c_qf, c_kf, c_vf, c_fl, c_g) = [col(i) for i in range(14)]
    ki_pad = jnp.concatenate([c_ki, jnp.zeros((depth, d, LANES - IDX_DIM), F32)], axis=2)
    w_rope = jnp.concatenate([c_qd, c_kd, c_qi, ki_pad], axis=2).astype(BF16)
    w_plain = jnp.concatenate([c_vd, c_zb, c_qf, c_kf, c_vf], axis=2).astype(BF16)
    w_z = jnp.concatenate([c_zc, c_zx], axis=2).astype(BF16)
    w_wi, w_fl, w_g = c_wi.astype(BF16), c_fl.astype(BF16), c_g.astype(BF16)
    w1gu, w1dn = w_ffn1_gu.astype(BF16), w_ffn1_dn.astype(BF16)
    w2gu, w2dn = w_ffn2_gu.astype(BF16), w_ffn2_dn.astype(BF16)
    wbd, wbc, wbf, wo = (w.astype(BF16) for w in (w_br_dsa, w_br_conv, w_br_fox, w_out))
    row = lambda g: g.reshape(depth, 1, -1)
    g1pre, g1post, gmpre, gmpost, g2pre, g2post, bfg = (
        row(g) for g in (g_ffn1_pre, g_ffn1_post, g_mix_pre, g_mix_post, g_ffn2_pre, g_ffn2_post, b_forget))

    pos = jnp.concatenate([jnp.tile(jnp.arange(seq, dtype=I32), bsz),
                           jnp.tile(past + jnp.arange(ts, dtype=I32), r)])
    cos_t, sa_t, sb_t = _rope_tables(pos)

    page_t = lambda c: jnp.moveaxis(c, 2, -1).reshape(depth, n_pool, -1, PAGE_SIZE)
    caches_t = [page_t(c) for c in (cache_dsa_k, cache_dsa_v, cache_idx_k, cache_fox_k, cache_fox_v, cache_fox_logf)]

    def chunks_t(a, w):
        return a[:n_p].reshape(bsz, nk, tk, w).transpose(0, 1, 3, 2)

    x = jnp.concatenate([x_prompt.reshape(n_p, d), x_sample.reshape(n_s, d)], axis=0)
    rows_p = [[] for _ in range(7)]
    rows_s = [[] for _ in range(7)]
    for l in range(depth):
        h = _ffn(x, g1pre, g1post, w1gu, w1dn, l, tm=tm, tf=tf)
        qd, kd, qi, ki, vd, zb, qf, kf, vf, z, wi, lf = _inproj(
            h, gmpre, cos_t, sa_t, sb_t, bfg, w_rope, w_plain, w_z, w_wi, w_fl, l, tm=tm)

        pr = lambda a: a[:n_p].reshape(bsz, seq, a.shape[1])
        sm = lambda a: a[n_p:].reshape(r, ts, a.shape[1])

        o_dsa = _dsa_prompt(qd, qi.T, wi.T, chunks_t(kd, KV_WIDTH), vd, ki, bsz, seq, tq=tq, tk=tk)
        lf_p = pr(lf)
        lf_blocks = lf_p.reshape(bsz, seq // LANES, LANES, FOX_HEADS).transpose(0, 1, 3, 2)
        f_k = _cumsum_prompt(lf_blocks.reshape(bsz, seq // LANES * FOX_HEADS, LANES))
        f_q = jnp.swapaxes(f_k, 1, 2)
        f_kc = f_k.reshape(bsz, FOX_HEADS, nk, tk).transpose(0, 2, 1, 3)
        o_fox = _fox_prompt(qf, chunks_t(kf, KV_WIDTH), vf, f_q, f_kc, bsz, seq, tq=tq, tk=tk)
        lf_new_t = jnp.pad(jnp.swapaxes(sm(lf), 1, 2), ((0, 0), (0, 0), (0, LANES - ts)))
        o_dsa, o_fox = _sample_attn(page_table, caches_t, qd, qi, wi, kd, vd, ki, qf, kf, vf, lf_new_t,
                                    o_dsa, o_fox, n_p, l)

        z_p, z_s = pr(z), sm(z)
        zp_p = jnp.concatenate([jnp.zeros((bsz, CONV_K - 1, CONV_WIDTH), F32), z_p], axis=1)
        zp_s = jnp.concatenate([state_conv[l].astype(F32), z_s], axis=1)
        taps = [jnp.concatenate([zp_p[:, j:j + seq].reshape(n_p, CONV_WIDTH),
                                 zp_s[:, j:j + ts].reshape(n_s, CONV_WIDTH)], axis=0) for j in range(CONV_K)]

        h = _merge(h, o_dsa, o_fox, zb, taps[0], taps[1], taps[2], gmpre, gmpost, conv_w,
                   w_g, wbd, wbc, wbf, wo, l, tm=tm)
        x = _ffn(h, g2pre, g2post, w2gu, w2dn, l, tm=tm, tf=tf)

        heads = lambda a: a.reshape(*a.shape[:-1], DSA_KV_HEADS, HEAD_DIM)
        for dst, take, zp, length in ((rows_p, pr, zp_p, seq), (rows_s, sm, zp_s, ts)):
            dst[0].append(heads(take(kd)))
            dst[1].append(heads(take(vd)))
            dst[2].append(take(ki))
            dst[3].append(heads(take(kf)))
            dst[4].append(heads(take(vf)))
            dst[5].append(take(lf))
            dst[6].append(zp[:, length:length + CONV_K - 1])

    sp = [jnp.stack(a, axis=0) for a in rows_p]
    ss = [jnp.stack(a, axis=0) for a in rows_s]
    return (x[:n_p].reshape(bsz, seq, d), x[n_p:].reshape(r, ts, d),
            sp[0], sp[1], sp[2], sp[3], sp[4], sp[5], sp[6],
            ss[0], ss[1], ss[2], ss[3], ss[4], ss[5], ss[6])
```

```python
import functools
import math

import jax
import jax.numpy as jnp
from jax import lax
from jax.experimental import pallas as pl
from jax.experimental.pallas import tpu as pltpu

HEAD_DIM = 64
DSA_HEADS = 8
DSA_KV_HEADS = 4
IDX_HEADS = 8
IDX_DIM = 64
TOPK_MAX = 256
CONV_WIDTH = 512
CONV_K = 3
FOX_HEADS = 8
FOX_KV_HEADS = 4
ROPE_THETA = 500000.0
ROPE_DIM = HEAD_DIM // 4
NORM_EPS = 1e-6
N_BRANCHES = 3
PAGE_SIZE = 128

LANES = 128
SUBLANES = 8
KV_WIDTH = DSA_KV_HEADS * HEAD_DIM
Q_WIDTH = DSA_HEADS * HEAD_DIM
VMEM_LIMIT = 56 * 1024 * 1024

F32 = jnp.float32
BF16 = jnp.bfloat16
I32 = jnp.int32
NEG = -1e30
INT_MIN = -(2 ** 31)
NEG_INF_KEY = INT_MIN + 0x7FFFFF
THR_ROW_GROUPS = 2

assert DSA_HEADS == FOX_HEADS and DSA_KV_HEADS == FOX_KV_HEADS
assert DSA_HEADS // DSA_KV_HEADS == 2 and 2 * HEAD_DIM == LANES


def _rms(x, g):
    return x * lax.rsqrt(jnp.mean(x * x, axis=-1, keepdims=True) + NORM_EPS) * g


def _nt_dot(a, b):
    return lax.dot_general(a, b, (((1,), (1,)), ((), ())), preferred_element_type=F32)


def _dot(a, b):
    return jnp.dot(a, b, preferred_element_type=F32)


def _params(*sem):
    return pltpu.CompilerParams(dimension_semantics=sem, vmem_limit_bytes=VMEM_LIMIT)


def _ffn_kernel(x_ref, gpre_ref, gpost_ref, wa_ref, wb_ref, wd_ref, o_ref, xn_ref, acc_ref, *, nf):
    j = pl.program_id(1)

    @pl.when(j == 0)
    def _():
        xn_ref[...] = _rms(x_ref[...], gpre_ref[...]).astype(BF16)
        acc_ref[...] = jnp.zeros_like(acc_ref)

    xn = xn_ref[...]
    a = _dot(xn, wa_ref[...])
    b = _dot(xn, wb_ref[...])
    act = (a * jax.nn.sigmoid(a) * b).astype(BF16)
    acc_ref[...] += _dot(act, wd_ref[...])

    @pl.when(j == nf - 1)
    def _():
        o_ref[...] = x_ref[...] + 0.5 * _rms(acc_ref[...], gpost_ref[...])


def _ffn(x, g_pre, g_post, w_gu, w_dn, l, *, tm, tf):
    n, d = x.shape
    dff = w_dn.shape[1]
    nf = dff // tf
    return pl.pallas_call(
        functools.partial(_ffn_kernel, nf=nf),
        grid=(n // tm, nf),
        in_specs=[
            pl.BlockSpec((tm, d), lambda i, j: (i, 0)),
            pl.BlockSpec((None, 1, d), lambda i, j: (l, 0, 0)),
            pl.BlockSpec((None, 1, d), lambda i, j: (l, 0, 0)),
            pl.BlockSpec((None, d, tf), lambda i, j: (l, 0, j)),
            pl.BlockSpec((None, d, tf), lambda i, j: (l, 0, j + nf)),
            pl.BlockSpec((None, tf, d), lambda i, j: (l, j, 0)),
        ],
        out_specs=pl.BlockSpec((tm, d), lambda i, j: (i, 0)),
        out_shape=jax.ShapeDtypeStruct((n, d), F32),
        scratch_shapes=[pltpu.VMEM((tm, d), BF16), pltpu.VMEM((tm, d), F32)],
        compiler_params=_params("parallel", "arbitrary"),
        name="ffn",
    )(x, g_pre, g_post, w_gu, w_gu, w_dn)


ROPE_COLS = Q_WIDTH + KV_WIDTH + Q_WIDTH + LANES
PLAIN_OFFSETS = (0, KV_WIDTH, KV_WIDTH + CONV_WIDTH, KV_WIDTH + CONV_WIDTH + Q_WIDTH,
                 2 * KV_WIDTH + CONV_WIDTH + Q_WIDTH, 3 * KV_WIDTH + CONV_WIDTH + Q_WIDTH)
PLAIN_COLS = PLAIN_OFFSETS[-1]


def _log_sigmoid(x):
    return jnp.minimum(x, 0.0) - jnp.log1p(jnp.exp(-jnp.abs(x)))


def _inproj_kernel(x_ref, g_ref, cos_ref, sa_ref, sb_ref, bf_ref, wr_ref, wp_ref, wz_ref, wwi_ref, wfl_ref,
                   qd_o, qi_o, zb_o, qf_o, z_o, wi_o, kd_p, vd_p, ki_p, kf_p, vf_p, lf_p,
                   kd_s, vd_s, ki_s, kf_s, vf_s, lf_s, *, prompt_tiles):
    xn = _rms(x_ref[...], g_ref[...]).astype(BF16)
    cos, sa, sb = cos_ref[...], sa_ref[...], sb_ref[...]

    r = _dot(xn, wr_ref[...])
    outs = []
    for c in range(ROPE_COLS // LANES):
        v = r[:, c * LANES:(c + 1) * LANES]
        up = pltpu.roll(v, LANES - ROPE_DIM // 2, axis=1)
        dn = pltpu.roll(v, ROPE_DIM // 2, axis=1)
        outs.append(v * cos + up * sa + dn * sb)
    nq = Q_WIDTH // LANES
    nk = KV_WIDTH // LANES
    qd_o[...] = jnp.concatenate(outs[:nq], axis=1)
    kd = jnp.concatenate(outs[nq:nq + nk], axis=1)
    qi_o[...] = jnp.concatenate(outs[nq + nk:2 * nq + nk], axis=1)
    ki = outs[2 * nq + nk][:, :IDX_DIM]

    p = _dot(xn, wp_ref[...])
    vd, zb, qf, kf, vf = (p[:, a:b] for a, b in zip(PLAIN_OFFSETS[:-1], PLAIN_OFFSETS[1:]))
    zb_o[...] = zb
    qf_o[...] = qf

    zz = _dot(xn, wz_ref[...])
    z_o[...] = zz[:, :CONV_WIDTH] * zz[:, CONV_WIDTH:]

    wi_o[...] = _dot(xn, wwi_ref[...]) * (IDX_HEADS ** -0.5)
    lf = _log_sigmoid(_dot(xn, wfl_ref[...]) + bf_ref[...])

    is_prompt = pl.program_id(0) < prompt_tiles
    for pred, refs in ((is_prompt, (kd_p, vd_p, ki_p, kf_p, vf_p, lf_p)),
                       (jnp.logical_not(is_prompt), (kd_s, vd_s, ki_s, kf_s, vf_s, lf_s))):
        @pl.when(pred)
        def _():
            for ref, val in zip(refs, (kd, vd, ki, kf, vf, lf)):
                ref[...] = val


def _inproj(x, g, cos, sa, sb, b_forget, wr, wp, wz, wwi, wfl, l, n_p, *, tm):
    n, d = x.shape
    p_tiles, s_tiles = n_p // tm, (n - n_p) // tm
    tok = lambda w: pl.BlockSpec((tm, w), lambda i: (i, 0))
    tok_p = lambda w: pl.BlockSpec((tm, w), lambda i: (jnp.minimum(i, p_tiles - 1), 0))
    tok_s = lambda w: pl.BlockSpec((tm, w), lambda i: (jnp.maximum(i - p_tiles, 0), 0))
    lay = lambda a, b: pl.BlockSpec((None, a, b), lambda i: (l, 0, 0))
    all_w = (Q_WIDTH, Q_WIDTH, CONV_WIDTH, Q_WIDTH, CONV_WIDTH, IDX_HEADS)
    row_w = (KV_WIDTH, KV_WIDTH, IDX_DIM, KV_WIDTH, KV_WIDTH, FOX_HEADS)
    shape = lambda rows, w: jax.ShapeDtypeStruct((rows, w), F32)
    outs = pl.pallas_call(
        functools.partial(_inproj_kernel, prompt_tiles=p_tiles),
        grid=(n // tm,),
        in_specs=[tok(d), lay(1, d), tok(LANES), tok(LANES), tok(LANES), lay(1, FOX_HEADS),
                  lay(d, ROPE_COLS), lay(d, PLAIN_COLS), lay(d, 2 * CONV_WIDTH),
                  lay(d, IDX_HEADS), lay(d, FOX_HEADS)],
        out_specs=[tok(w) for w in all_w] + [tok_p(w) for w in row_w] + [tok_s(w) for w in row_w],
        out_shape=([shape(n, w) for w in all_w] + [shape(n_p, w) for w in row_w]
                   + [shape(s_tiles * tm, w) for w in row_w]),
        compiler_params=_params("arbitrary"),
        name="inproj",
    )(x, g, cos, sa, sb, b_forget, wr, wp, wz, wwi, wfl)
    return outs[:6], outs[6:12], outs[12:]


def _block_cumsum(x, n_blocks):
    rows = n_blocks * SUBLANES
    padded = -(-rows // LANES) * LANES
    if padded != rows:
        x = jnp.concatenate([x, jnp.zeros((padded - rows, LANES), F32)], axis=0)
    hi = functools.partial(jnp.dot, preferred_element_type=F32, precision=lax.Precision.HIGHEST)
    r0 = lax.broadcasted_iota(I32, (LANES, LANES), 0)
    c0 = lax.broadcasted_iota(I32, (LANES, LANES), 1)
    within = hi(x, jnp.where(r0 <= c0, 1.0, 0.0).astype(F32))
    totals = jnp.broadcast_to(within[:, LANES - 1:LANES], (padded, LANES))
    dist = lax.broadcasted_iota(I32, (padded, padded), 0) - lax.broadcasted_iota(I32, (padded, padded), 1)
    earlier = jnp.where((dist > 0) & ((dist & (SUBLANES - 1)) == 0), 1.0, 0.0).astype(F32)
    full = within + hi(earlier, totals)
    return [full[b * SUBLANES:(b + 1) * SUBLANES] for b in range(n_blocks)]


def _cumsum_kernel(x_ref, o_ref, *, n_blocks):
    o_ref[...] = jnp.concatenate(_block_cumsum(x_ref[...], n_blocks), axis=1)


def _cumsum_prompt(lf_blocks):
    b, rows, _ = lf_blocks.shape
    n_blocks = rows // FOX_HEADS
    t = n_blocks * LANES
    return pl.pallas_call(
        functools.partial(_cumsum_kernel, n_blocks=n_blocks),
        grid=(b,),
        in_specs=[pl.BlockSpec((None, rows, LANES), lambda i: (i, 0, 0))],
        out_specs=pl.BlockSpec((None, FOX_HEADS, t), lambda i: (i, 0, 0)),
        out_shape=jax.ShapeDtypeStruct((b, FOX_HEADS, t), F32),
        compiler_params=_params("parallel"),
        name="cumsum_prompt",
    )(lf_blocks)


def _q_block_diag(q):
    t = q.shape[0]
    lane = lax.broadcasted_iota(I32, (t, LANES), 1)
    zeros = jnp.zeros((t, LANES), F32)
    blocks = []
    for h in range(DSA_HEADS):
        g, r = divmod(h, 2)
        src = q[:, g * LANES:(g + 1) * LANES]
        if r != g % 2:
            src = pltpu.roll(src, HEAD_DIM, axis=1)
        keep = (lane < HEAD_DIM) if g % 2 == 0 else (lane >= HEAD_DIM)
        m = jnp.where(keep, src, 0.0)
        blocks.append(jnp.concatenate([m, zeros] if g // 2 == 0 else [zeros, m], axis=1))
    return jnp.concatenate(blocks, axis=0).astype(BF16)


def _head_rows(q):
    return jnp.concatenate([q[:, h * IDX_DIM:(h + 1) * IDX_DIM] for h in range(IDX_HEADS)], axis=0)


def _extract_heads(acc, t):
    outs = []
    for h in range(DSA_HEADS):
        g = h // 2
        outs.append(acc[h * t:(h + 1) * t, g * HEAD_DIM:(g + 1) * HEAD_DIM])
    return jnp.concatenate(outs, axis=1)


def _sort_key(score):
    bits = pltpu.bitcast(score, I32)
    return jnp.where(bits < 0, bits ^ 0x7FFFFFFF, bits)


def _indexer_score(s, wi, t):
    score = jnp.zeros((t, s.shape[1]), F32)
    for h in range(s.shape[0] // t):
        score = score + jnp.maximum(s[h * t:(h + 1) * t], 0.0) * wi[:, h:h + 1]
    return jnp.where(score == 0.0, 0.0, score)


def _fold_lanes(x):
    out = x[:, :LANES]
    for j in range(1, x.shape[1] // LANES):
        out = out + x[:, j * LANES:(j + 1) * LANES]
    return out


LOG2E = 1.4426950408889634


def _two_pass_attention(qbd, kt_ref, v_ref, loops, bias_fn, s_ref, mx_ref, l_ref, acc_ref, tq, tk):
    group = DSA_HEADS // DSA_KV_HEADS
    n_tiles = tk // LANES

    def fold(x, op):
        out = x[:, :, :LANES]
        for j in range(1, n_tiles):
            out = op(out, x[:, :, j * LANES:(j + 1) * LANES])
        return out

    def logit_chunk(c, carry, *, tag):
        kt = kt_ref[c].astype(BF16)
        bias = bias_fn(c, tag)
        for g in range(DSA_HEADS // group):
            h0, h1 = g * group, (g + 1) * group
            s3 = _dot(qbd[h0 * tq:h1 * tq], kt).reshape(group, tq, tk) + bias(h0, h1)
            s_ref[c, h0:h1] = s3
            mx_ref[h0:h1] = jnp.maximum(mx_ref[h0:h1], fold(s3, jnp.maximum))
        return carry

    def pv_chunk(c, carry):
        v = v_ref[pl.ds(pl.multiple_of(c * tk, tk), tk), :].astype(BF16)
        for g in range(DSA_HEADS // group):
            h0, h1 = g * group, (g + 1) * group
            m = jnp.concatenate([mx_ref[h0:h1]] * n_tiles, axis=2)
            p = jnp.exp2(s_ref[c, h0:h1] - m)
            l_ref[h0:h1] += fold(p, jnp.add)
            acc_ref[h0:h1] += _dot(p.reshape(group * tq, tk).astype(BF16), v).reshape(group, tq, KV_WIDTH)
        return carry

    mx_ref[...] = jnp.full_like(mx_ref, NEG)
    for lo, hi, tag in loops:
        lax.fori_loop(lo, hi, functools.partial(logit_chunk, tag=tag), 0)
    mx_ref[...] = jnp.broadcast_to(jnp.max(mx_ref[...], axis=2, keepdims=True), mx_ref.shape)
    l_ref[...] = jnp.zeros_like(l_ref)
    acc_ref[...] = jnp.zeros_like(acc_ref)
    lax.fori_loop(loops[0][0], loops[-1][1], pv_chunk, 0)
    out = acc_ref[...] / jnp.sum(l_ref[...], axis=2, keepdims=True)
    return _extract_heads(out.reshape(DSA_HEADS * tq, KV_WIDTH), tq)


def _two_pass_scratch(tq, tk, nk):
    return [pltpu.VMEM((nk, DSA_HEADS, tq, tk), F32), pltpu.VMEM((DSA_HEADS, tq, LANES), F32),
            pltpu.VMEM((DSA_HEADS, tq, LANES), F32), pltpu.VMEM((DSA_HEADS, tq, KV_WIDTH), F32)]


def _fox_prompt_kernel(q_ref, kt_ref, v_ref, fq_ref, fk_ref, o_ref, s_ref, m_ref, l_ref, acc_ref, *, tq, tk):
    i = pl.program_id(1)
    qbd = _q_block_diag(q_ref[...] * (HEAD_DIM ** -0.5 * LOG2E))
    fq = fq_ref[...] * LOG2E
    fq_rep = jnp.stack([jnp.broadcast_to(fq[:, h:h + 1], (tq, LANES)) for h in range(FOX_HEADS)], axis=0)
    qpos = i * tq + lax.broadcasted_iota(I32, (tq, tk), 0)
    kiota = lax.broadcasted_iota(I32, (tq, tk), 1)

    def bias(c, masked):
        fk = fk_ref[c] * LOG2E
        fk3 = jnp.stack([fk[h:h + 1, :] for h in range(FOX_HEADS)], axis=0)
        causal = ((kiota + c * tk) <= qpos)[None]

        def heads(h0, h1):
            b = jnp.concatenate([fq_rep[h0:h1]] * (tk // LANES), axis=2) - fk3[h0:h1]
            return jnp.where(causal, b, NEG) if masked else b

        return heads

    n_full = (i * tq + 1) // tk
    loops = ((0, n_full, False), (n_full, (i * tq + tq - 1) // tk + 1, True))
    o_ref[...] = _two_pass_attention(qbd, kt_ref, v_ref, loops, bias, s_ref, m_ref, l_ref, acc_ref, tq, tk)


def _fox_prompt(q, kt, v, fq, fk, bsz, seq, *, tq, tk):
    n = q.shape[0]
    nq, nk = seq // tq, seq // tk
    return pl.pallas_call(
        functools.partial(_fox_prompt_kernel, tq=tq, tk=tk),
        grid=(bsz, nq),
        in_specs=[
            pl.BlockSpec((tq, Q_WIDTH), lambda bi, i: (bi * nq + i, 0)),
            pl.BlockSpec((None, nk, KV_WIDTH, tk), lambda bi, i: (bi, 0, 0, 0)),
            pl.BlockSpec((seq, KV_WIDTH), lambda bi, i: (bi, 0)),
            pl.BlockSpec((None, tq, FOX_HEADS), lambda bi, i: (bi, i, 0)),
            pl.BlockSpec((None, nk, FOX_HEADS, tk), lambda bi, i: (bi, 0, 0, 0)),
        ],
        out_specs=pl.BlockSpec((tq, Q_WIDTH), lambda bi, i: (bi * nq + i, 0)),
        out_shape=jax.ShapeDtypeStruct((n, Q_WIDTH), F32),
        scratch_shapes=_two_pass_scratch(tq, tk, nk),
        compiler_params=_params("parallel", "parallel"),
        name="fox_prompt",
    )(q, kt, v, fq, fk)


def _dsa_prompt_kernel(q_ref, qit_ref, wit_ref, kt_ref, v_ref, ki_ref, o_ref,
                       key_ref, thr_ref, j_ref, s_ref, m_ref, l_ref, acc_ref, *, tq, tk, top_k, seq):
    i = pl.program_id(1)
    n_vis = (i * tq + tq - 1) // tk + 1
    kpos = lax.broadcasted_iota(I32, (tk, tq), 0)
    qpos = i * tq + lax.broadcasted_iota(I32, (tk, tq), 1)
    kf = float(top_k)

    qit = qit_ref[...].astype(BF16)
    qi_t = jnp.concatenate([qit[h * IDX_DIM:(h + 1) * IDX_DIM] for h in range(IDX_HEADS)], axis=1)
    wit = wit_ref[...] * (IDX_DIM ** -0.5)

    def score_chunk(c, carry):
        start = pl.multiple_of(c * tk, tk)
        s = _dot(ki_ref[pl.ds(start, tk), :].astype(BF16), qi_t)
        score = jnp.zeros((tk, tq), F32)
        for h in range(IDX_HEADS):
            score = score + jnp.maximum(s[:, h * tq:(h + 1) * tq], 0.0) * wit[h:h + 1, :]
        score = jnp.where(score == 0.0, 0.0, score)
        score = jnp.where((kpos + c * tk) <= qpos, score, -jnp.inf)
        key_ref[c] = _sort_key(score)
        return carry

    lax.fori_loop(0, n_vis, score_chunk, 0)

    def partial_count(mask):
        ones = jnp.where(mask, 1.0, 0.0).reshape(8, tk // (8 * SUBLANES), SUBLANES, tq)
        c = [jnp.sum(ones[g], axis=0) for g in range(8)]
        return ((c[0] + c[1]) + (c[2] + c[3])) + ((c[4] + c[5]) + (c[6] + c[7]))

    def count(*preds):
        def body(c, cnts):
            key, pos = key_ref[c], kpos + c * tk
            return tuple(cnt + partial_count(pred(key, pos)) for cnt, pred in zip(cnts, preds))
        zero = jnp.zeros((SUBLANES, tq), F32)
        totals = lax.fori_loop(0, n_vis, body, tuple(zero for _ in preds))
        return tuple(jnp.sum(x, axis=0, keepdims=True) for x in totals)

    def search(n_chunks):
        def step(b, cand):
            trial = cand ^ lax.shift_left(jnp.int32(1), 31 - b)
            cnt = jnp.zeros((SUBLANES, tq), F32)
            for c in range(n_chunks):
                cnt = cnt + partial_count(key_ref[c] >= trial)
            return jnp.where(jnp.sum(cnt, axis=0, keepdims=True) >= kf, trial, cand)

        return lax.fori_loop(0, 32, step, jnp.full((1, tq), INT_MIN, I32))

    for n_chunks in range(1, seq // tk + 1):
        @pl.when(n_vis == n_chunks)
        def _():
            thr_ref[...] = search(n_chunks)

    thr = thr_ref[...]

    n_gt, n_ge = count(lambda key, pos: key > thr, lambda key, pos: key >= thr)
    need = kf - n_gt
    j_ref[...] = jnp.full_like(j_ref, seq)
    tied = jnp.where((n_ge > kf) & (thr != NEG_INF_KEY), 1.0, 0.0)

    @pl.when(jnp.max(tied) > 0.0)
    def _():
        def idx_step(b, lo):
            trial = lo | lax.shift_left(jnp.int32(1), int(math.log2(seq)) - 1 - b)
            (c,) = count(lambda key, pos: jnp.where(key == thr, pos, seq) < trial)
            return jnp.where(c < need, trial, lo)

        j_ref[...] = lax.fori_loop(0, int(math.log2(seq)), idx_step, jnp.zeros((1, tq), I32))

    jmax = j_ref[...]

    qbd = _q_block_diag(q_ref[...] * (HEAD_DIM ** -0.5 * LOG2E))

    def bias(c, tag):
        key = key_ref[c]
        tie_bias = jnp.where(jnp.where(key == thr, kpos + c * tk, seq + 1) <= jmax, 0.0, NEG)
        b = jnp.where(key > NEG_INF_KEY, jnp.where(key > thr, 0.0, tie_bias), NEG).T[None]
        return lambda h0, h1: b

    o_ref[...] = _two_pass_attention(qbd, kt_ref, v_ref, ((0, n_vis, None),), bias, s_ref, m_ref, l_ref, acc_ref,
                                     tq, tk)


def _dsa_prompt(q, qit, wit, kt, v, ki, bsz, seq, *, tq, tk):
    n = q.shape[0]
    nq, nk = seq // tq, seq // tk
    top_k = min(TOPK_MAX, seq // 4)
    return pl.pallas_call(
        functools.partial(_dsa_prompt_kernel, tq=tq, tk=tk, top_k=top_k, seq=seq),
        grid=(bsz, nq),
        in_specs=[
            pl.BlockSpec((tq, Q_WIDTH), lambda bi, i: (bi * nq + i, 0)),
            pl.BlockSpec((Q_WIDTH, tq), lambda bi, i: (0, bi * nq + i)),
            pl.BlockSpec((IDX_HEADS, tq), lambda bi, i: (0, bi * nq + i)),
            pl.BlockSpec((None, nk, KV_WIDTH, tk), lambda bi, i: (bi, 0, 0, 0)),
            pl.BlockSpec((seq, KV_WIDTH), lambda bi, i: (bi, 0)),
            pl.BlockSpec((seq, IDX_DIM), lambda bi, i: (bi, 0)),
        ],
        out_specs=pl.BlockSpec((tq, Q_WIDTH), lambda bi, i: (bi * nq + i, 0)),
        out_shape=jax.ShapeDtypeStruct((n, Q_WIDTH), F32),
        scratch_shapes=[pltpu.VMEM((nk, tk, tq), I32), pltpu.VMEM((1, tq), I32), pltpu.VMEM((1, tq), I32)]
        + _two_pass_scratch(tq, tk, nk),
        compiler_params=_params("parallel", "parallel"),
        name="dsa_prompt",
    )(q, qit, wit, kt, v, ki)


def _page_specs(n_pages, rows, l):
    return [pl.BlockSpec((None, None, rows, PAGE_SIZE),
                         functools.partial(lambda b, pt, p: (l, pt[b * n_pages + p], 0, 0), p=p))
            for p in range(n_pages)]


def _pad_rows(x, rows):
    return jnp.concatenate([x, jnp.zeros((rows - x.shape[0], x.shape[1]), x.dtype)], axis=0)


def _paged_logits(lhs, pages_t, new_rows):
    blocks = [_dot(lhs, p[...].astype(BF16)) for p in pages_t]
    blocks.append(_nt_dot(lhs, _pad_rows(new_rows, PAGE_SIZE).astype(BF16)))
    return jnp.concatenate(blocks, axis=1)


def _softmax_pv(logits, v_pages_t, v_new, t):
    m = jnp.max(logits, axis=1, keepdims=True)
    p = jnp.exp(logits - m)
    l = jnp.sum(p, axis=1, keepdims=True)
    pb = p.astype(BF16)
    acc = jnp.zeros((logits.shape[0], KV_WIDTH), F32)
    for n, vp in enumerate(v_pages_t):
        acc = acc + _nt_dot(pb[:, n * PAGE_SIZE:(n + 1) * PAGE_SIZE], vp[...].astype(BF16))
    n = len(v_pages_t)
    acc = acc + _dot(pb[:, n * PAGE_SIZE:(n + 1) * PAGE_SIZE], _pad_rows(v_new, PAGE_SIZE).astype(BF16))
    return _extract_heads(acc / l, t)


def _visible(t, length, past):
    pos = lax.broadcasted_iota(I32, (t, length), 1)
    return pos, pos <= past + lax.broadcasted_iota(I32, (t, length), 0)


def _fox_sample_body(kp, vp, lp, q_ref, kn_ref, vn_ref, lfn_ref, t):
    n_pages = len(kp)
    past = n_pages * PAGE_SIZE
    length = past + PAGE_SIZE

    f_tiles = _block_cumsum(jnp.concatenate([p[...] for p in lp] + [lfn_ref[...]], axis=0), n_pages + 1)
    fk = jnp.concatenate(f_tiles, axis=1)
    lane = lax.broadcasted_iota(I32, (t, LANES), 1)
    diag = lane == lax.broadcasted_iota(I32, (t, LANES), 0)

    qbd = _q_block_diag(q_ref[...] * (HEAD_DIM ** -0.5))
    s = _paged_logits(qbd, kp, kn_ref[...])
    _, visible = _visible(t, length, past)
    rows = []
    for h in range(FOX_HEADS):
        f_new = jnp.broadcast_to(f_tiles[n_pages][h:h + 1, :], (t, LANES))
        fq = jnp.sum(jnp.where(diag, f_new, 0.0), axis=1, keepdims=True)
        sh = s[h * t:(h + 1) * t] + (fq - fk[h:h + 1, :])
        rows.append(jnp.where(visible, sh, NEG))
    return _softmax_pv(jnp.concatenate(rows, axis=0), vp, vn_ref[...], t)


def _sample_topk(key, pos, top_k, length, j_ref):
    t = key.shape[0]
    kf = float(top_k)
    count = lambda mask: jnp.sum(jnp.where(mask, 1.0, 0.0), axis=1, keepdims=True)

    def digit_pass(cand, shift, n_trials):
        digit = jnp.zeros((t, 1), I32)
        for j in range(1, n_trials + 1):
            trial = cand + jnp.left_shift(jnp.int32(j), shift)
            digit = digit + jnp.where(count(key >= trial) >= kf, 1, 0)
        return cand + jnp.left_shift(digit, shift)

    thr = digit_pass(jnp.full((t, 1), INT_MIN, I32), 30, 3)
    for shift in range(27, -1, -3):
        thr = digit_pass(thr, shift, 7)

    need = kf - count(key > thr)
    n_ge = count(key >= thr)
    j_ref[...] = jnp.full_like(j_ref, length)
    tied = jnp.where((n_ge > kf) & (thr != NEG_INF_KEY), 1.0, 0.0)

    @pl.when(jnp.max(tied) > 0.0)
    def _():
        n_bits = int(math.ceil(math.log2(length)))
        tie_pos = jnp.where(key == thr, pos, length)

        def idx_step(b, lo):
            trial = lo | lax.shift_left(jnp.int32(1), n_bits - 1 - b)
            return jnp.where(count(tie_pos < trial) < need, trial, lo)

        j_ref[...] = lax.fori_loop(0, n_bits, idx_step, jnp.zeros((t, 1), I32))

    return thr, j_ref[...]


def _dsa_sample_body(kp, vp, ip, q_ref, qi_ref, wi_ref, kn_ref, vn_ref, in_ref, j_ref, t, top_k):
    n_pages = len(kp)
    past = n_pages * PAGE_SIZE
    length = past + PAGE_SIZE

    qi = _head_rows(qi_ref[...]).astype(BF16)
    score = _indexer_score(_paged_logits(qi, ip, in_ref[...]), wi_ref[...] * (IDX_DIM ** -0.5), t)
    pos, visible = _visible(t, length, past)
    key = _sort_key(jnp.where(visible, score, -jnp.inf))
    thr, jmax = _sample_topk(key, pos, top_k, length, j_ref)
    tie_bias = jnp.where(jnp.where(key == thr, pos, length + 1) <= jmax, 0.0, NEG)
    bias = jnp.where(key > NEG_INF_KEY, jnp.where(key > thr, 0.0, tie_bias), NEG)

    qbd = _q_block_diag(q_ref[...] * (HEAD_DIM ** -0.5))
    logits = _paged_logits(qbd, kp, kn_ref[...]) + jnp.concatenate([bias] * DSA_HEADS, axis=0)
    return _softmax_pv(logits, vp, vn_ref[...], t)


def _sample_attn_kernel(pt_ref, *refs, n_pages, t, top_k):
    pages = [refs[n * n_pages:(n + 1) * n_pages] for n in range(6)]
    kd, vd, ki, kf, vf, lf = pages
    (qd_ref, qi_ref, wi_ref, kdn_ref, vdn_ref, kin_ref, qf_ref, kfn_ref, vfn_ref, lfn_ref, _, _,
     od_ref, of_ref, j_ref) = refs[6 * n_pages:]
    of_ref[...] = _fox_sample_body(kf, vf, lf, qf_ref, kfn_ref, vfn_ref, lfn_ref, t)
    od_ref[...] = _dsa_sample_body(kd, vd, ki, qd_ref, qi_ref, wi_ref, kdn_ref, vdn_ref, kin_ref, j_ref, t, top_k)


def _sample_attn(page_table, caches_t, qd, qi, wi, kd, vd, ki, qf, kf, vf, lf_new_t, o_dsa, o_fox, n_p, l):
    r, n_pages = page_table.shape
    t = (qd.shape[0] - n_p) // r
    off = n_p // t
    top_k = min(TOPK_MAX, (n_pages * PAGE_SIZE + t) // 4)
    req = lambda w: pl.BlockSpec((t, w), lambda bi, pt: (off + bi, 0))
    page_in = []
    for c in caches_t:
        page_in += _page_specs(n_pages, c.shape[2], l)
    new = lambda w: pl.BlockSpec((t, w), lambda bi, pt: (bi, 0))
    row_specs = [req(Q_WIDTH), req(Q_WIDTH), req(IDX_HEADS), new(KV_WIDTH), new(KV_WIDTH), new(IDX_DIM),
                 req(Q_WIDTH), new(KV_WIDTH), new(KV_WIDTH)]
    grid_spec = pltpu.PrefetchScalarGridSpec(
        num_scalar_prefetch=1,
        grid=(r,),
        in_specs=(page_in + row_specs
                  + [pl.BlockSpec((None, FOX_HEADS, LANES), lambda bi, pt: (bi, 0, 0)),
                     pl.BlockSpec(memory_space=pl.ANY), pl.BlockSpec(memory_space=pl.ANY)]),
        out_specs=[req(Q_WIDTH), req(Q_WIDTH)],
        scratch_shapes=[pltpu.VMEM((t, 1), I32)],
    )
    n_in = 1 + 6 * n_pages + len(row_specs) + 3
    page_args = [c for c in caches_t for _ in range(n_pages)]
    return pl.pallas_call(
        functools.partial(_sample_attn_kernel, n_pages=n_pages, t=t, top_k=top_k),
        grid_spec=grid_spec,
        out_shape=[jax.ShapeDtypeStruct(o_dsa.shape, F32), jax.ShapeDtypeStruct(o_fox.shape, F32)],
        input_output_aliases={n_in - 2: 0, n_in - 1: 1},
        compiler_params=_params("parallel"),
        name="sample_attn",
    )(page_table.reshape(-1), *page_args, qd, qi, wi, kd, vd, ki, qf, kf, vf, lf_new_t, o_dsa, o_fox)


def _merge_kernel(h_ref, od_ref, of_ref, zb_ref, z_ref, zprev_ref, e1_ref, e2_ref, gpre_ref, gpost_ref, cw_ref,
                  wg_ref, wbd_ref, wbc_ref, wbf_ref, wo_ref, o_ref, *, prompt_tiles, seq_tiles, ts):
    h = h_ref[...]
    tm, d = h.shape
    xn = _rms(h, gpre_ref[...]).astype(BF16)

    i = pl.program_id(0)
    is_sample = i >= prompt_tiles
    z = z_ref[...]
    row = lax.broadcasted_iota(I32, z.shape, 0)
    t = row & (jnp.where(is_sample, ts, tm) - 1)
    keep = jnp.where(jnp.logical_or(is_sample, i % seq_tiles == 0), 0.0, 1.0)
    prev = zprev_ref[...] * keep
    p1, p2 = prev[SUBLANES - 1:SUBLANES], prev[SUBLANES - 2:SUBLANES - 1]
    r1 = jnp.where(is_sample, e1_ref[...], jnp.broadcast_to(p1, z.shape))
    r2 = jnp.where(is_sample, e2_ref[...], jnp.where(t == 0, p2, p1))
    z1 = jnp.where(t == 0, r1, pltpu.roll(z, 1, axis=0))
    z2 = jnp.where(t < 2, r2, pltpu.roll(z, 2, axis=0))
    cw = cw_ref[...]
    conv = cw[0:1] * z2 + cw[1:2] * z1 + cw[2:3] * z

    branches = ((od_ref[...], wbd_ref), (zb_ref[...] * conv, wbc_ref), (of_ref[...], wbf_ref))
    merged = jnp.zeros_like(h)
    for n, (val, w_ref) in enumerate(branches):
        gate = jax.nn.sigmoid(_dot(xn, wg_ref[:, n * d:(n + 1) * d]))
        merged = merged + gate * _dot(val.astype(BF16), w_ref[...])
    o_ref[...] = h + _rms(_dot(merged.astype(BF16), wo_ref[...]), gpost_ref[...])


def _merge(h, od, of, zb, z, e1, e2, g_pre, g_post, conv_w, wg, wbd, wbc, wbf, wo, l, n_p, seq, ts, *, tm):
    n, d = h.shape
    assert seq % tm == 0 and tm % ts == 0 and tm & (tm - 1) == 0 and ts & (ts - 1) == 0 and CONV_K == 3
    p_tiles = n_p // tm
    tok = lambda w: pl.BlockSpec((tm, w), lambda i: (i, 0))
    lay = lambda a, b: pl.BlockSpec((None, a, b), lambda i: (l, 0, 0))
    halo = pl.BlockSpec((SUBLANES, CONV_WIDTH), lambda i: (jnp.maximum(i * (tm // SUBLANES) - 1, 0), 0))
    smp = pl.BlockSpec((tm, CONV_WIDTH), lambda i: (jnp.maximum(i - p_tiles, 0), 0))
    return pl.pallas_call(
        functools.partial(_merge_kernel, prompt_tiles=p_tiles, seq_tiles=seq // tm, ts=ts),
        grid=(n // tm,),
        in_specs=[tok(d), tok(Q_WIDTH), tok(Q_WIDTH), tok(CONV_WIDTH), tok(CONV_WIDTH), halo, smp, smp,
                  lay(1, d), lay(1, d), lay(CONV_K, CONV_WIDTH),
                  lay(d, N_BRANCHES * d), lay(Q_WIDTH, d), lay(CONV_WIDTH, d), lay(Q_WIDTH, d), lay(d, d)],
        out_specs=tok(d),
        out_shape=jax.ShapeDtypeStruct((n, d), F32),
        compiler_params=_params("parallel"),
        name="merge",
    )(h, od, of, zb, z, z, e1, e2, g_pre, g_post, conv_w, wg, wbd, wbc, wbf, wo)


def _rope_tables(pos):
    half = ROPE_DIM // 2
    inv = jnp.power(jnp.float32(ROPE_THETA), -jnp.arange(half, dtype=jnp.float32) * (2.0 / ROPE_DIM))
    ang = pos.astype(jnp.float32)[:, None] * inv[None, :]
    cos, sin = jnp.cos(ang), jnp.sin(ang)
    n = pos.shape[0]
    ones = jnp.ones((n, HEAD_DIM - ROPE_DIM), F32)
    zeros_r = jnp.zeros((n, HEAD_DIM - ROPE_DIM), F32)
    zeros_h = jnp.zeros((n, half), F32)
    cos_t = jnp.concatenate([cos, cos, ones], axis=1)
    sa_t = jnp.concatenate([-sin, zeros_h, zeros_r], axis=1)
    sb_t = jnp.concatenate([zeros_h, sin, zeros_r], axis=1)
    rep = LANES // HEAD_DIM
    return tuple(jnp.tile(x, (1, rep)) for x in (cos_t, sa_t, sb_t))


def _pick(n, prefs):
    for p in prefs:
        if n % p == 0:
            return p
    return n


def kernel(x_prompt, x_sample, cache_dsa_k, cache_dsa_v, cache_idx_k, cache_fox_k, cache_fox_v, cache_fox_logf,
           state_conv, page_table, g_ffn1_pre, g_ffn1_post, w_ffn1_gu, w_ffn1_dn, g_mix_pre, g_mix_post, w_in,
           b_forget, conv_w, w_br_dsa, w_br_conv, w_br_fox, w_out, g_ffn2_pre, g_ffn2_post, w_ffn2_gu, w_ffn2_dn):
    bsz, seq, d = x_prompt.shape
    r, ts, _ = x_sample.shape
    depth = w_in.shape[0]
    n_pool = cache_dsa_k.shape[1]
    n_pages = page_table.shape[1]
    past = n_pages * PAGE_SIZE
    n_p, n_s = bsz * seq, r * ts
    dff = w_ffn1_dn.shape[1]
    assert ts == SUBLANES and seq % LANES == 0

    tm = _pick(math.gcd(n_p, n_s), (512, 256, 128, 64, 32, 16, 8))
    tf = _pick(dff, (1408, 1024, 512, 256, 128))
    tq = _pick(seq, (128,))
    tk = _pick(seq, (512, 256, 128))
    nk = seq // tk

    sizes = (Q_WIDTH, KV_WIDTH, KV_WIDTH, IDX_HEADS * IDX_DIM, IDX_DIM, IDX_HEADS, CONV_WIDTH, CONV_WIDTH,
             CONV_WIDTH, Q_WIDTH, KV_WIDTH, KV_WIDTH, FOX_HEADS, N_BRANCHES * d)
    offs = [0]
    for s_ in sizes:
        offs.append(offs[-1] + s_)
    col = lambda i: w_in[:, :, offs[i]:offs[i + 1]]
    (c_qd, c_kd, c_vd, c_qi, c_ki, c_wi, c_zb, c_zc, c_zx, c_qf, c_kf, c_vf, c_fl, c_g) = [col(i) for i in range(14)]
    ki_pad = jnp.concatenate([c_ki, jnp.zeros((depth, d, LANES - IDX_DIM), F32)], axis=2)
    w_rope = jnp.concatenate([c_qd, c_kd, c_qi, ki_pad], axis=2).astype(BF16)
    w_plain = jnp.concatenate([c_vd, c_zb, c_qf, c_kf, c_vf], axis=2).astype(BF16)
    w_z = jnp.concatenate([c_zc, c_zx], axis=2).astype(BF16)
    w_wi, w_fl, w_g = c_wi.astype(BF16), c_fl.astype(BF16), c_g.astype(BF16)
    w1gu, w1dn = w_ffn1_gu.astype(BF16), w_ffn1_dn.astype(BF16)
    w2gu, w2dn = w_ffn2_gu.astype(BF16), w_ffn2_dn.astype(BF16)
    wbd, wbc, wbf, wo = (w.astype(BF16) for w in (w_br_dsa, w_br_conv, w_br_fox, w_out))
    row = lambda g: g.reshape(depth, 1, -1)
    g1pre, g1post, gmpre, gmpost, g2pre, g2post, bfg = (
        row(g) for g in (g_ffn1_pre, g_ffn1_post, g_mix_pre, g_mix_post, g_ffn2_pre, g_ffn2_post, b_forget))

    pos = jnp.concatenate([jnp.tile(jnp.arange(seq, dtype=I32), bsz),
                           jnp.tile(past + jnp.arange(ts, dtype=I32), r)])
    cos_t, sa_t, sb_t = _rope_tables(pos)

    page_t = lambda c: jnp.moveaxis(c, 2, -1).reshape(depth, n_pool, -1, PAGE_SIZE)
    caches_t = [page_t(c) for c in (cache_dsa_k, cache_dsa_v, cache_idx_k, cache_fox_k, cache_fox_v, cache_fox_logf)]

    def chunks_t(a, w):
        return a.reshape(bsz, nk, tk, w).transpose(0, 1, 3, 2)

    x = jnp.concatenate([x_prompt.reshape(n_p, d), x_sample.reshape(n_s, d)], axis=0)
    rows_p = [[] for _ in range(7)]
    rows_s = [[] for _ in range(7)]
    for l in range(depth):
        h = _ffn(x, g1pre, g1post, w1gu, w1dn, l, tm=tm, tf=tf)
        (qd, qi, zb, qf, z, wi), new_p, new_s = _inproj(
            h, gmpre, cos_t, sa_t, sb_t, bfg, w_rope, w_plain, w_z, w_wi, w_fl, l, n_p, tm=tm)
        kd_p, vd_p, ki_p, kf_p, vf_p, lf_p = new_p
        kd_s, vd_s, ki_s, kf_s, vf_s, lf_s = new_s

        o_dsa = _dsa_prompt(qd, qi.T, wi.T, chunks_t(kd_p, KV_WIDTH), vd_p, ki_p, bsz, seq, tq=tq, tk=tk)
        lf_blocks = lf_p.reshape(bsz, seq // LANES, LANES, FOX_HEADS).transpose(0, 1, 3, 2)
        f_k = _cumsum_prompt(lf_blocks.reshape(bsz, seq // LANES * FOX_HEADS, LANES))
        f_q = jnp.swapaxes(f_k, 1, 2)
        f_kc = f_k.reshape(bsz, FOX_HEADS, nk, tk).transpose(0, 2, 1, 3)
        o_fox = _fox_prompt(qf, chunks_t(kf_p, KV_WIDTH), vf_p, f_q, f_kc, bsz, seq, tq=tq, tk=tk)
        lf_new_t = jnp.pad(jnp.swapaxes(lf_s.reshape(r, ts, FOX_HEADS), 1, 2), ((0, 0), (0, 0), (0, LANES - ts)))
        o_dsa, o_fox = _sample_attn(page_table, caches_t, qd, qi, wi, kd_s, vd_s, ki_s, qf, kf_s, vf_s, lf_new_t,
                                    o_dsa, o_fox, n_p, l)

        st = state_conv[l].astype(F32)
        pad_t = lambda a: jnp.pad(a, ((0, 0), (0, ts - a.shape[1]), (0, 0))).reshape(n_s, CONV_WIDTH)
        h = _merge(h, o_dsa, o_fox, zb, z, pad_t(st[:, 1:]), pad_t(st), gmpre, gmpost, conv_w,
                   w_g, wbd, wbc, wbf, wo, l, n_p, seq, ts, tm=tm)
        x = _ffn(h, g2pre, g2post, w2gu, w2dn, l, tm=tm, tf=tf)

        tails = (z[:n_p].reshape(bsz, seq, CONV_WIDTH)[:, seq - (CONV_K - 1):],
                 z[n_p:].reshape(r, ts, CONV_WIDTH)[:, ts - (CONV_K - 1):])
        for dst, new, lead, tail in ((rows_p, new_p, (bsz, seq), tails[0]), (rows_s, new_s, (r, ts), tails[1])):
            kd_, vd_, ki_, kf_, vf_, lf_ = new
            for n, a in enumerate((kd_, vd_, None, kf_, vf_)):
                if a is not None:
                    dst[n].append(a.reshape(*lead, DSA_KV_HEADS, HEAD_DIM))
            dst[2].append(ki_.reshape(*lead, IDX_DIM))
            dst[5].append(lf_.reshape(*lead, FOX_HEADS))
            dst[6].append(tail)

    sp = [jnp.stack(a, axis=0) for a in rows_p]
    ss = [jnp.stack(a, axis=0) for a in rows_s]
    return (x[:n_p].reshape(bsz, seq, d), x[n_p:].reshape(r, ts, d),
            sp[0], sp[1], sp[2], sp[3], sp[4], sp[5], sp[6],
            ss[0], ss[1], ss[2], ss[3], ss[4], ss[5], ss[6])
```

```python
import functools
import math

import jax
import jax.numpy as jnp
from jax import lax
from jax.experimental import pallas as pl
from jax.experimental.pallas import tpu as pltpu

HEAD_DIM = 64
DSA_HEADS = 8
DSA_KV_HEADS = 4
IDX_HEADS = 8
IDX_DIM = 64
TOPK_MAX = 256
CONV_WIDTH = 512
CONV_K = 3
FOX_HEADS = 8
FOX_KV_HEADS = 4
ROPE_THETA = 500000.0
ROPE_DIM = HEAD_DIM // 4
NORM_EPS = 1e-6
N_BRANCHES = 3
PAGE_SIZE = 128

LANES = 128
SUBLANES = 8
KV_WIDTH = DSA_KV_HEADS * HEAD_DIM
Q_WIDTH = DSA_HEADS * HEAD_DIM
VMEM_LIMIT = 56 * 1024 * 1024

F32 = jnp.float32
BF16 = jnp.bfloat16
I32 = jnp.int32
NEG = -1e30
INT_MIN = -(2 ** 31)
NEG_INF_KEY = INT_MIN + 0x7FFFFF
SEARCH_STAGES = (0, 24, 28, 32)

assert DSA_HEADS == FOX_HEADS and DSA_KV_HEADS == FOX_KV_HEADS
assert DSA_HEADS // DSA_KV_HEADS == 2 and 2 * HEAD_DIM == LANES


def _rms(x, g):
    return x * lax.rsqrt(jnp.mean(x * x, axis=-1, keepdims=True) + NORM_EPS) * g


def _nt_dot(a, b):
    return lax.dot_general(a, b, (((1,), (1,)), ((), ())), preferred_element_type=F32)


def _dot(a, b):
    return jnp.dot(a, b, preferred_element_type=F32)


def _params(*sem):
    return pltpu.CompilerParams(dimension_semantics=sem, vmem_limit_bytes=VMEM_LIMIT)


def _ffn_kernel(x_ref, gpre_ref, gpost_ref, wa_ref, wb_ref, wd_ref, o_ref, xn_ref, acc_ref, *, nf):
    j = pl.program_id(1)

    @pl.when(j == 0)
    def _():
        xn_ref[...] = _rms(x_ref[...], gpre_ref[...]).astype(BF16)
        acc_ref[...] = jnp.zeros_like(acc_ref)

    xn = xn_ref[...]
    a = _dot(xn, wa_ref[...])
    b = _dot(xn, wb_ref[...])
    act = (a * jax.nn.sigmoid(a) * b).astype(BF16)
    acc_ref[...] += _dot(act, wd_ref[...])

    @pl.when(j == nf - 1)
    def _():
        o_ref[...] = x_ref[...] + 0.5 * _rms(acc_ref[...], gpost_ref[...])


def _ffn(x, g_pre, g_post, w_gu, w_dn, l, *, tm, tf):
    n, d = x.shape
    dff = w_dn.shape[1]
    nf = dff // tf
    return pl.pallas_call(
        functools.partial(_ffn_kernel, nf=nf),
        grid=(n // tm, nf),
        in_specs=[
            pl.BlockSpec((tm, d), lambda i, j: (i, 0)),
            pl.BlockSpec((None, 1, d), lambda i, j: (l, 0, 0)),
            pl.BlockSpec((None, 1, d), lambda i, j: (l, 0, 0)),
            pl.BlockSpec((None, d, tf), lambda i, j: (l, 0, j)),
            pl.BlockSpec((None, d, tf), lambda i, j: (l, 0, j + nf)),
            pl.BlockSpec((None, tf, d), lambda i, j: (l, j, 0)),
        ],
        out_specs=pl.BlockSpec((tm, d), lambda i, j: (i, 0)),
        out_shape=jax.ShapeDtypeStruct((n, d), F32),
        scratch_shapes=[pltpu.VMEM((tm, d), BF16), pltpu.VMEM((tm, d), F32)],
        compiler_params=_params("parallel", "arbitrary"),
        name="ffn",
    )(x, g_pre, g_post, w_gu, w_gu, w_dn)


ROPE_COLS = Q_WIDTH + KV_WIDTH + Q_WIDTH + LANES
PLAIN_OFFSETS = (0, KV_WIDTH, KV_WIDTH + CONV_WIDTH, KV_WIDTH + CONV_WIDTH + Q_WIDTH,
                 2 * KV_WIDTH + CONV_WIDTH + Q_WIDTH, 3 * KV_WIDTH + CONV_WIDTH + Q_WIDTH)
PLAIN_COLS = PLAIN_OFFSETS[-1]


def _log_sigmoid(x):
    return jnp.minimum(x, 0.0) - jnp.log1p(jnp.exp(-jnp.abs(x)))


def _inproj_kernel(x_ref, g_ref, cos_ref, sa_ref, sb_ref, bf_ref, wr_ref, wp_ref, wz_ref, wwi_ref, wfl_ref,
                   *refs, prompt_tiles):
    (qd_o, qi_o, zb_o, qf_o, z_o, wi_o, kd_l, kf_l, kd_p, vd_p, ki_p, kf_p, vf_p, lf_p,
     kd_s, vd_s, ki_s, kf_s, vf_s, lf_s) = refs[-20:]
    xn = _rms(x_ref[...], g_ref[...]).astype(BF16)
    cos, sa, sb = cos_ref[...], sa_ref[...], sb_ref[...]

    r = _dot(xn, wr_ref[...])
    outs = []
    for c in range(ROPE_COLS // LANES):
        v = r[:, c * LANES:(c + 1) * LANES]
        up = pltpu.roll(v, LANES - ROPE_DIM // 2, axis=1)
        dn = pltpu.roll(v, ROPE_DIM // 2, axis=1)
        outs.append(v * cos + up * sa + dn * sb)
    nq = Q_WIDTH // LANES
    nk = KV_WIDTH // LANES
    qd_o[...] = jnp.concatenate(outs[:nq], axis=1)
    kd = jnp.concatenate(outs[nq:nq + nk], axis=1)
    qi_o[...] = jnp.concatenate(outs[nq + nk:2 * nq + nk], axis=1)
    ki = outs[2 * nq + nk][:, :IDX_DIM]

    p = _dot(xn, wp_ref[...])
    vd, zb, qf, kf, vf = (p[:, a:b] for a, b in zip(PLAIN_OFFSETS[:-1], PLAIN_OFFSETS[1:]))
    zb_o[...] = zb
    qf_o[...] = qf

    zz = _dot(xn, wz_ref[...])
    z_o[...] = zz[:, :CONV_WIDTH] * zz[:, CONV_WIDTH:]

    wi_o[...] = _dot(xn, wwi_ref[...]) * (IDX_HEADS ** -0.5)
    lf = _log_sigmoid(_dot(xn, wfl_ref[...]) + bf_ref[...])

    is_prompt = pl.program_id(0) < prompt_tiles
    for pred, refs in ((is_prompt, (kd_p, vd_p, ki_p, kf_p, vf_p, lf_p, kd_l, kf_l)),
                       (jnp.logical_not(is_prompt), (kd_s, vd_s, ki_s, kf_s, vf_s, lf_s))):
        @pl.when(pred)
        def _():
            for ref, val in zip(refs, (kd, vd, ki, kf, vf, lf, kd, kf)):
                ref[...] = val


def _inproj(x, g, cos, sa, sb, b_forget, wr, wp, wz, wwi, wfl, l, n_p, stacks, *, tm):
    n, d = x.shape
    depth = wr.shape[0]
    p_tiles, s_tiles = n_p // tm, (n - n_p) // tm
    tok = lambda w: pl.BlockSpec((tm, w), lambda i: (i, 0))
    tok_p = lambda w: pl.BlockSpec((None, tm, w), lambda i: (l, jnp.minimum(i, p_tiles - 1), 0))
    tok_s = lambda w: pl.BlockSpec((None, tm, w), lambda i: (l, jnp.maximum(i - p_tiles, 0), 0))
    tok_l = pl.BlockSpec((tm, KV_WIDTH), lambda i: (jnp.minimum(i, p_tiles - 1), 0))
    lay = lambda a, b: pl.BlockSpec((None, a, b), lambda i: (l, 0, 0))
    all_w = (Q_WIDTH, Q_WIDTH, CONV_WIDTH, Q_WIDTH, CONV_WIDTH, IDX_HEADS)
    row_w = (KV_WIDTH, KV_WIDTH, IDX_DIM, KV_WIDTH, KV_WIDTH, FOX_HEADS)
    in_specs = [tok(d), lay(1, d), tok(LANES), tok(LANES), tok(LANES), lay(1, FOX_HEADS),
                lay(d, ROPE_COLS), lay(d, PLAIN_COLS), lay(d, 2 * CONV_WIDTH), lay(d, IDX_HEADS), lay(d, FOX_HEADS)]
    args = [x, g, cos, sa, sb, b_forget, wr, wp, wz, wwi, wfl]
    aliases = {}
    if stacks is not None:
        aliases = {len(args) + k: len(all_w) + 2 + k for k in range(len(stacks))}
        in_specs = in_specs + [pl.BlockSpec(memory_space=pl.ANY)] * len(stacks)
        args = args + list(stacks)
    outs = pl.pallas_call(
        functools.partial(_inproj_kernel, prompt_tiles=p_tiles),
        grid=(n // tm,),
        in_specs=in_specs,
        out_specs=([tok(w) for w in all_w] + [tok_l, tok_l] + [tok_p(w) for w in row_w]
                   + [tok_s(w) for w in row_w]),
        out_shape=([jax.ShapeDtypeStruct((n, w), F32) for w in all_w]
                   + [jax.ShapeDtypeStruct((n_p, KV_WIDTH), F32)] * 2
                   + [jax.ShapeDtypeStruct((depth, n_p, w), F32) for w in row_w]
                   + [jax.ShapeDtypeStruct((depth, s_tiles * tm, w), F32) for w in row_w]),
        input_output_aliases=aliases,
        compiler_params=_params("arbitrary"),
        name="inproj",
    )(*args)
    return outs[:6], outs[6:8], outs[8:]


def _block_cumsum(x, n_blocks):
    rows = n_blocks * SUBLANES
    padded = -(-rows // LANES) * LANES
    if padded != rows:
        x = jnp.concatenate([x, jnp.zeros((padded - rows, LANES), F32)], axis=0)
    hi = functools.partial(jnp.dot, preferred_element_type=F32, precision=lax.Precision.HIGHEST)
    r0 = lax.broadcasted_iota(I32, (LANES, LANES), 0)
    c0 = lax.broadcasted_iota(I32, (LANES, LANES), 1)
    within = hi(x, jnp.where(r0 <= c0, 1.0, 0.0).astype(F32))
    totals = jnp.broadcast_to(within[:, LANES - 1:LANES], (padded, LANES))
    dist = lax.broadcasted_iota(I32, (padded, padded), 0) - lax.broadcasted_iota(I32, (padded, padded), 1)
    earlier = jnp.where((dist > 0) & ((dist & (SUBLANES - 1)) == 0), 1.0, 0.0).astype(F32)
    full = within + hi(earlier, totals)
    return [full[b * SUBLANES:(b + 1) * SUBLANES] for b in range(n_blocks)]


def _cumsum_kernel(x_ref, o_ref, *, n_blocks):
    o_ref[...] = jnp.concatenate(_block_cumsum(x_ref[...], n_blocks), axis=1)


def _cumsum_prompt(lf_blocks):
    b, rows, _ = lf_blocks.shape
    n_blocks = rows // FOX_HEADS
    t = n_blocks * LANES
    return pl.pallas_call(
        functools.partial(_cumsum_kernel, n_blocks=n_blocks),
        grid=(b,),
        in_specs=[pl.BlockSpec((None, rows, LANES), lambda i: (i, 0, 0))],
        out_specs=pl.BlockSpec((None, FOX_HEADS, t), lambda i: (i, 0, 0)),
        out_shape=jax.ShapeDtypeStruct((b, FOX_HEADS, t), F32),
        compiler_params=_params("parallel"),
        name="cumsum_prompt",
    )(lf_blocks)


def _q_block_diag(q):
    t = q.shape[0]
    lane = lax.broadcasted_iota(I32, (t, LANES), 1)
    zeros = jnp.zeros((t, LANES), F32)
    blocks = []
    for h in range(DSA_HEADS):
        g, r = divmod(h, 2)
        src = q[:, g * LANES:(g + 1) * LANES]
        if r != g % 2:
            src = pltpu.roll(src, HEAD_DIM, axis=1)
        keep = (lane < HEAD_DIM) if g % 2 == 0 else (lane >= HEAD_DIM)
        m = jnp.where(keep, src, 0.0)
        blocks.append(jnp.concatenate([m, zeros] if g // 2 == 0 else [zeros, m], axis=1))
    return jnp.concatenate(blocks, axis=0).astype(BF16)


def _head_rows(q):
    return jnp.concatenate([q[:, h * IDX_DIM:(h + 1) * IDX_DIM] for h in range(IDX_HEADS)], axis=0)


def _extract_heads(acc, t):
    outs = []
    for h in range(DSA_HEADS):
        g = h // 2
        outs.append(acc[h * t:(h + 1) * t, g * HEAD_DIM:(g + 1) * HEAD_DIM])
    return jnp.concatenate(outs, axis=1)


def _sort_key(score):
    bits = pltpu.bitcast(score, I32)
    return jnp.where(bits < 0, bits ^ 0x7FFFFFFF, bits)


def _indexer_score(s, wi, t):
    score = jnp.zeros((t, s.shape[1]), F32)
    for h in range(s.shape[0] // t):
        score = score + jnp.maximum(s[h * t:(h + 1) * t], 0.0) * wi[:, h:h + 1]
    return jnp.where(score == 0.0, 0.0, score)


def _fold_lanes(x):
    out = x[:, :LANES]
    for j in range(1, x.shape[1] // LANES):
        out = out + x[:, j * LANES:(j + 1) * LANES]
    return out


LOG2E = 1.4426950408889634


def _two_pass_attention(qbd, kt_ref, v_ref, loops, bias_fn, s_ref, mx_ref, l_ref, acc_ref, tq, tk):
    group = DSA_HEADS // DSA_KV_HEADS
    n_tiles = tk // LANES

    def fold(x, op):
        out = x[:, :, :LANES]
        for j in range(1, n_tiles):
            out = op(out, x[:, :, j * LANES:(j + 1) * LANES])
        return out

    def logit_chunk(c, carry, *, tag):
        kt = kt_ref[c].astype(BF16)
        bias = bias_fn(c, tag)
        for g in range(DSA_HEADS // group):
            h0, h1 = g * group, (g + 1) * group
            s3 = _dot(qbd[h0 * tq:h1 * tq], kt).reshape(group, tq, tk) + bias(h0, h1)
            s_ref[c, h0:h1] = s3
            mx_ref[h0:h1] = jnp.maximum(mx_ref[h0:h1], fold(s3, jnp.maximum))
        return carry

    def pv_chunk(c, carry):
        v = v_ref[pl.ds(pl.multiple_of(c * tk, tk), tk), :].astype(BF16)
        for g in range(DSA_HEADS // group):
            h0, h1 = g * group, (g + 1) * group
            m = jnp.concatenate([mx_ref[h0:h1]] * n_tiles, axis=2)
            p = jnp.exp2(s_ref[c, h0:h1] - m)
            l_ref[h0:h1] += fold(p, jnp.add)
            acc_ref[h0:h1] += _dot(p.reshape(group * tq, tk).astype(BF16), v).reshape(group, tq, KV_WIDTH)
        return carry

    mx_ref[...] = jnp.full_like(mx_ref, NEG)
    for lo, hi, tag in loops:
        lax.fori_loop(lo, hi, functools.partial(logit_chunk, tag=tag), 0)
    mx_ref[...] = jnp.broadcast_to(jnp.max(mx_ref[...], axis=2, keepdims=True), mx_ref.shape)
    l_ref[...] = jnp.zeros_like(l_ref)
    acc_ref[...] = jnp.zeros_like(acc_ref)
    lax.fori_loop(loops[0][0], loops[-1][1], pv_chunk, 0)
    out = acc_ref[...] / jnp.sum(l_ref[...], axis=2, keepdims=True)
    return _extract_heads(out.reshape(DSA_HEADS * tq, KV_WIDTH), tq)


def _two_pass_scratch(tq, tk, nk):
    return [pltpu.VMEM((nk, DSA_HEADS, tq, tk), F32), pltpu.VMEM((DSA_HEADS, tq, LANES), F32),
            pltpu.VMEM((DSA_HEADS, tq, LANES), F32), pltpu.VMEM((DSA_HEADS, tq, KV_WIDTH), F32)]


def _fox_prompt_kernel(q_ref, kt_ref, v_ref, fq_ref, fk_ref, o_ref, s_ref, m_ref, l_ref, acc_ref, *, tq, tk):
    i = pl.program_id(1)
    qbd = _q_block_diag(q_ref[...] * (HEAD_DIM ** -0.5 * LOG2E))
    fq = fq_ref[...] * LOG2E
    fq_rep = jnp.stack([jnp.broadcast_to(fq[:, h:h + 1], (tq, LANES)) for h in range(FOX_HEADS)], axis=0)
    qpos = i * tq + lax.broadcasted_iota(I32, (tq, tk), 0)
    kiota = lax.broadcasted_iota(I32, (tq, tk), 1)

    def bias(c, masked):
        fk = fk_ref[c] * LOG2E
        fk3 = jnp.stack([fk[h:h + 1, :] for h in range(FOX_HEADS)], axis=0)
        causal = ((kiota + c * tk) <= qpos)[None]

        def heads(h0, h1):
            b = jnp.concatenate([fq_rep[h0:h1]] * (tk // LANES), axis=2) - fk3[h0:h1]
            return jnp.where(causal, b, NEG) if masked else b

        return heads

    n_full = (i * tq + 1) // tk
    loops = ((0, n_full, False), (n_full, (i * tq + tq - 1) // tk + 1, True))
    o_ref[...] = _two_pass_attention(qbd, kt_ref, v_ref, loops, bias, s_ref, m_ref, l_ref, acc_ref, tq, tk)


def _fox_prompt(q, kt, v, fq, fk, bsz, seq, l, *, tq, tk):
    n = q.shape[0]
    nq, nk = seq // tq, seq // tk
    return pl.pallas_call(
        functools.partial(_fox_prompt_kernel, tq=tq, tk=tk),
        grid=(bsz, nq),
        in_specs=[
            pl.BlockSpec((tq, Q_WIDTH), lambda bi, i: (bi * nq + i, 0)),
            pl.BlockSpec((None, nk, KV_WIDTH, tk), lambda bi, i: (bi, 0, 0, 0)),
            pl.BlockSpec((None, seq, KV_WIDTH), lambda bi, i: (l, bi, 0)),
            pl.BlockSpec((None, tq, FOX_HEADS), lambda bi, i: (bi, i, 0)),
            pl.BlockSpec((None, nk, FOX_HEADS, tk), lambda bi, i: (bi, 0, 0, 0)),
        ],
        out_specs=pl.BlockSpec((tq, Q_WIDTH), lambda bi, i: (bi * nq + i, 0)),
        out_shape=jax.ShapeDtypeStruct((n, Q_WIDTH), F32),
        scratch_shapes=_two_pass_scratch(tq, tk, nk),
        compiler_params=_params("parallel", "parallel"),
        name="fox_prompt",
    )(q, kt, v, fq, fk)


def _dsa_prompt_kernel(q_ref, qit_ref, wit_ref, kt_ref, v_ref, ki_ref, o_ref,
                       key_ref, thr_ref, cnt_ref, j_ref, s_ref, m_ref, l_ref, acc_ref, *, tq, tk, top_k, seq):
    i = pl.program_id(1)
    n_vis = (i * tq + tq - 1) // tk + 1
    kpos = lax.broadcasted_iota(I32, (tk, tq), 0)
    qpos = i * tq + lax.broadcasted_iota(I32, (tk, tq), 1)
    kf = float(top_k)

    qit = qit_ref[...].astype(BF16)
    qi_t = jnp.concatenate([qit[h * IDX_DIM:(h + 1) * IDX_DIM] for h in range(IDX_HEADS)], axis=1)
    wit = wit_ref[...] * (IDX_DIM ** -0.5)

    def score_chunk(c, carry):
        start = pl.multiple_of(c * tk, tk)
        s = _dot(ki_ref[pl.ds(start, tk), :].astype(BF16), qi_t)
        score = jnp.zeros((tk, tq), F32)
        for h in range(IDX_HEADS):
            score = score + jnp.maximum(s[:, h * tq:(h + 1) * tq], 0.0) * wit[h:h + 1, :]
        score = jnp.where(score == 0.0, 0.0, score)
        score = jnp.where((kpos + c * tk) <= qpos, score, -jnp.inf)
        key_ref[c] = _sort_key(score)
        return carry

    lax.fori_loop(0, n_vis, score_chunk, 0)

    def partial_count(mask):
        ones = jnp.where(mask, 1.0, 0.0).reshape(8, tk // (8 * SUBLANES), SUBLANES, tq)
        c = [jnp.sum(ones[g], axis=0) for g in range(8)]
        return ((c[0] + c[1]) + (c[2] + c[3])) + ((c[4] + c[5]) + (c[6] + c[7]))

    def count(*preds):
        def body(c, cnts):
            key, pos = key_ref[c], kpos + c * tk
            return tuple(cnt + partial_count(pred(key, pos)) for cnt, pred in zip(cnts, preds))
        zero = jnp.zeros((SUBLANES, tq), F32)
        totals = lax.fori_loop(0, n_vis, body, tuple(zero for _ in preds))
        return tuple(jnp.sum(x, axis=0, keepdims=True) for x in totals)

    def search(n_chunks):
        def step(b, carry):
            cand, cnt_at = carry
            trial = cand ^ lax.shift_left(jnp.int32(1), 31 - b)
            cnt = jnp.zeros((SUBLANES, tq), F32)
            for c in range(n_chunks):
                cnt = cnt + partial_count(key_ref[c] >= trial)
            total = jnp.sum(cnt, axis=0, keepdims=True)
            ok = total >= kf
            return jnp.where(ok, trial, cand), jnp.where(ok, total, cnt_at)

        carry = (jnp.full((1, tq), INT_MIN, I32), jnp.full((1, tq), float(n_chunks * tk), F32))
        for lo, hi in zip(SEARCH_STAGES[:-1], SEARCH_STAGES[1:]):
            if lo == 0:
                carry = lax.fori_loop(lo, hi, step, carry)
                thr_ref[...], cnt_ref[...] = carry
            else:
                @pl.when(jnp.max(jnp.where(cnt_ref[...] != kf, 1.0, 0.0)) > 0.0)
                def _():
                    thr_ref[...], cnt_ref[...] = lax.fori_loop(lo, hi, step, (thr_ref[...], cnt_ref[...]))

    for n_chunks in range(1, seq // tk + 1):
        @pl.when(n_vis == n_chunks)
        def _():
            search(n_chunks)

    thr = thr_ref[...]

    n_gt, n_ge = count(lambda key, pos: key > thr, lambda key, pos: key >= thr)
    need = kf - n_gt
    j_ref[...] = jnp.full_like(j_ref, seq)
    tied = jnp.where((n_ge > kf) & (thr != NEG_INF_KEY), 1.0, 0.0)

    @pl.when(jnp.max(tied) > 0.0)
    def _():
        def idx_step(b, lo):
            trial = lo | lax.shift_left(jnp.int32(1), int(math.log2(seq)) - 1 - b)
            (c,) = count(lambda key, pos: jnp.where(key == thr, pos, seq) < trial)
            return jnp.where(c < need, trial, lo)

        j_ref[...] = lax.fori_loop(0, int(math.log2(seq)), idx_step, jnp.zeros((1, tq), I32))

    jmax = j_ref[...]

    qbd = _q_block_diag(q_ref[...] * (HEAD_DIM ** -0.5 * LOG2E))

    def bias(c, tag):
        key = key_ref[c]
        tie_bias = jnp.where(jnp.where(key == thr, kpos + c * tk, seq + 1) <= jmax, 0.0, NEG)
        b = jnp.where(key > NEG_INF_KEY, jnp.where(key > thr, 0.0, tie_bias), NEG).T[None]
        return lambda h0, h1: b

    o_ref[...] = _two_pass_attention(qbd, kt_ref, v_ref, ((0, n_vis, None),), bias, s_ref, m_ref, l_ref, acc_ref,
                                     tq, tk)


def _dsa_prompt(q, qit, wit, kt, v, ki, bsz, seq, l, *, tq, tk):
    n = q.shape[0]
    nq, nk = seq // tq, seq // tk
    top_k = min(TOPK_MAX, seq // 4)
    return pl.pallas_call(
        functools.partial(_dsa_prompt_kernel, tq=tq, tk=tk, top_k=top_k, seq=seq),
        grid=(bsz, nq),
        in_specs=[
            pl.BlockSpec((tq, Q_WIDTH), lambda bi, i: (bi * nq + i, 0)),
            pl.BlockSpec((Q_WIDTH, tq), lambda bi, i: (0, bi * nq + i)),
            pl.BlockSpec((IDX_HEADS, tq), lambda bi, i: (0, bi * nq + i)),
            pl.BlockSpec((None, nk, KV_WIDTH, tk), lambda bi, i: (bi, 0, 0, 0)),
            pl.BlockSpec((None, seq, KV_WIDTH), lambda bi, i: (l, bi, 0)),
            pl.BlockSpec((None, seq, IDX_DIM), lambda bi, i: (l, bi, 0)),
        ],
        out_specs=pl.BlockSpec((tq, Q_WIDTH), lambda bi, i: (bi * nq + i, 0)),
        out_shape=jax.ShapeDtypeStruct((n, Q_WIDTH), F32),
        scratch_shapes=[pltpu.VMEM((nk, tk, tq), I32), pltpu.VMEM((1, tq), I32), pltpu.VMEM((1, tq), F32),
                        pltpu.VMEM((1, tq), I32)]
        + _two_pass_scratch(tq, tk, nk),
        compiler_params=_params("parallel", "parallel"),
        name="dsa_prompt",
    )(q, qit, wit, kt, v, ki)


def _page_specs(n_pages, rows, l):
    return [pl.BlockSpec((None, None, rows, PAGE_SIZE),
                         functools.partial(lambda b, pt, p: (l, pt[b * n_pages + p], 0, 0), p=p))
            for p in range(n_pages)]


def _pad_rows(x, rows):
    return jnp.concatenate([x, jnp.zeros((rows - x.shape[0], x.shape[1]), x.dtype)], axis=0)


def _paged_logits(lhs, pages_t, new_rows):
    blocks = [_dot(lhs, p[...].astype(BF16)) for p in pages_t]
    blocks.append(_nt_dot(lhs, _pad_rows(new_rows, PAGE_SIZE).astype(BF16)))
    return jnp.concatenate(blocks, axis=1)


def _softmax_pv(logits, v_pages_t, v_new, t):
    m = jnp.max(logits, axis=1, keepdims=True)
    p = jnp.exp(logits - m)
    l = jnp.sum(p, axis=1, keepdims=True)
    pb = p.astype(BF16)
    acc = jnp.zeros((logits.shape[0], KV_WIDTH), F32)
    for n, vp in enumerate(v_pages_t):
        acc = acc + _nt_dot(pb[:, n * PAGE_SIZE:(n + 1) * PAGE_SIZE], vp[...].astype(BF16))
    n = len(v_pages_t)
    acc = acc + _dot(pb[:, n * PAGE_SIZE:(n + 1) * PAGE_SIZE], _pad_rows(v_new, PAGE_SIZE).astype(BF16))
    return _extract_heads(acc / l, t)


def _visible(t, length, past):
    pos = lax.broadcasted_iota(I32, (t, length), 1)
    return pos, pos <= past + lax.broadcasted_iota(I32, (t, length), 0)


def _fox_sample_body(kp, vp, lp, q_ref, kn_ref, vn_ref, lfn_ref, t):
    n_pages = len(kp)
    past = n_pages * PAGE_SIZE
    length = past + PAGE_SIZE

    f_tiles = _block_cumsum(jnp.concatenate([p[...] for p in lp] + [lfn_ref[...]], axis=0), n_pages + 1)
    fk = jnp.concatenate(f_tiles, axis=1)
    lane = lax.broadcasted_iota(I32, (t, LANES), 1)
    diag = lane == lax.broadcasted_iota(I32, (t, LANES), 0)

    qbd = _q_block_diag(q_ref[...] * (HEAD_DIM ** -0.5))
    s = _paged_logits(qbd, kp, kn_ref[...])
    _, visible = _visible(t, length, past)
    rows = []
    for h in range(FOX_HEADS):
        f_new = jnp.broadcast_to(f_tiles[n_pages][h:h + 1, :], (t, LANES))
        fq = jnp.sum(jnp.where(diag, f_new, 0.0), axis=1, keepdims=True)
        sh = s[h * t:(h + 1) * t] + (fq - fk[h:h + 1, :])
        rows.append(jnp.where(visible, sh, NEG))
    return _softmax_pv(jnp.concatenate(rows, axis=0), vp, vn_ref[...], t)


def _sample_topk(key, pos, top_k, length, j_ref):
    t = key.shape[0]
    kf = float(top_k)
    count = lambda mask: jnp.sum(jnp.where(mask, 1.0, 0.0), axis=1, keepdims=True)

    def digit_pass(cand, shift, n_trials):
        digit = jnp.zeros((t, 1), I32)
        for j in range(1, n_trials + 1):
            trial = cand + jnp.left_shift(jnp.int32(j), shift)
            digit = digit + jnp.where(count(key >= trial) >= kf, 1, 0)
        return cand + jnp.left_shift(digit, shift)

    thr = digit_pass(jnp.full((t, 1), INT_MIN, I32), 30, 3)
    for shift in range(27, -1, -3):
        thr = digit_pass(thr, shift, 7)

    need = kf - count(key > thr)
    n_ge = count(key >= thr)
    j_ref[...] = jnp.full_like(j_ref, length)
    tied = jnp.where((n_ge > kf) & (thr != NEG_INF_KEY), 1.0, 0.0)

    @pl.when(jnp.max(tied) > 0.0)
    def _():
        n_bits = int(math.ceil(math.log2(length)))
        tie_pos = jnp.where(key == thr, pos, length)

        def idx_step(b, lo):
            trial = lo | lax.shift_left(jnp.int32(1), n_bits - 1 - b)
            return jnp.where(count(tie_pos < trial) < need, trial, lo)

        j_ref[...] = lax.fori_loop(0, n_bits, idx_step, jnp.zeros((t, 1), I32))

    return thr, j_ref[...]


def _dsa_sample_body(kp, vp, ip, q_ref, qi_ref, wi_ref, kn_ref, vn_ref, in_ref, j_ref, t, top_k):
    n_pages = len(kp)
    past = n_pages * PAGE_SIZE
    length = past + PAGE_SIZE

    qi = _head_rows(qi_ref[...]).astype(BF16)
    score = _indexer_score(_paged_logits(qi, ip, in_ref[...]), wi_ref[...] * (IDX_DIM ** -0.5), t)
    pos, visible = _visible(t, length, past)
    key = _sort_key(jnp.where(visible, score, -jnp.inf))
    thr, jmax = _sample_topk(key, pos, top_k, length, j_ref)
    tie_bias = jnp.where(jnp.where(key == thr, pos, length + 1) <= jmax, 0.0, NEG)
    bias = jnp.where(key > NEG_INF_KEY, jnp.where(key > thr, 0.0, tie_bias), NEG)

    qbd = _q_block_diag(q_ref[...] * (HEAD_DIM ** -0.5))
    logits = _paged_logits(qbd, kp, kn_ref[...]) + jnp.concatenate([bias] * DSA_HEADS, axis=0)
    return _softmax_pv(logits, vp, vn_ref[...], t)


def _sample_attn_kernel(pt_ref, *refs, n_pages, t, top_k):
    pages = [refs[n * n_pages:(n + 1) * n_pages] for n in range(6)]
    kd, vd, ki, kf, vf, lf = pages
    (qd_ref, qi_ref, wi_ref, kdn_ref, vdn_ref, kin_ref, qf_ref, kfn_ref, vfn_ref, lfn_ref, _, _,
     od_ref, of_ref, j_ref) = refs[6 * n_pages:]
    of_ref[...] = _fox_sample_body(kf, vf, lf, qf_ref, kfn_ref, vfn_ref, lfn_ref, t)
    od_ref[...] = _dsa_sample_body(kd, vd, ki, qd_ref, qi_ref, wi_ref, kdn_ref, vdn_ref, kin_ref, j_ref, t, top_k)


def _sample_attn(page_table, caches_t, qd, qi, wi, kd, vd, ki, qf, kf, vf, lf_new_t, o_dsa, o_fox, n_p, l):
    r, n_pages = page_table.shape
    t = (qd.shape[0] - n_p) // r
    off = n_p // t
    top_k = min(TOPK_MAX, (n_pages * PAGE_SIZE + t) // 4)
    req = lambda w: pl.BlockSpec((t, w), lambda bi, pt: (off + bi, 0))
    page_in = []
    for c in caches_t:
        page_in += _page_specs(n_pages, c.shape[2], l)
    new = lambda w: pl.BlockSpec((None, t, w), lambda bi, pt: (l, bi, 0))
    row_specs = [req(Q_WIDTH), req(Q_WIDTH), req(IDX_HEADS), new(KV_WIDTH), new(KV_WIDTH), new(IDX_DIM),
                 req(Q_WIDTH), new(KV_WIDTH), new(KV_WIDTH)]
    grid_spec = pltpu.PrefetchScalarGridSpec(
        num_scalar_prefetch=1,
        grid=(r,),
        in_specs=(page_in + row_specs
                  + [pl.BlockSpec((None, FOX_HEADS, LANES), lambda bi, pt: (bi, 0, 0)),
                     pl.BlockSpec(memory_space=pl.ANY), pl.BlockSpec(memory_space=pl.ANY)]),
        out_specs=[req(Q_WIDTH), req(Q_WIDTH)],
        scratch_shapes=[pltpu.VMEM((t, 1), I32)],
    )
    n_in = 1 + 6 * n_pages + len(row_specs) + 3
    page_args = [c for c in caches_t for _ in range(n_pages)]
    return pl.pallas_call(
        functools.partial(_sample_attn_kernel, n_pages=n_pages, t=t, top_k=top_k),
        grid_spec=grid_spec,
        out_shape=[jax.ShapeDtypeStruct(o_dsa.shape, F32), jax.ShapeDtypeStruct(o_fox.shape, F32)],
        input_output_aliases={n_in - 2: 0, n_in - 1: 1},
        compiler_params=_params("parallel"),
        name="sample_attn",
    )(page_table.reshape(-1), *page_args, qd, qi, wi, kd, vd, ki, qf, kf, vf, lf_new_t, o_dsa, o_fox)


def _merge_kernel(h_ref, od_ref, of_ref, zb_ref, z_ref, zprev_ref, e1_ref, e2_ref, gpre_ref, gpost_ref, cw_ref,
                  wg_ref, wbd_ref, wbc_ref, wbf_ref, wo_ref, o_ref, *, prompt_tiles, seq_tiles, ts):
    h = h_ref[...]
    tm, d = h.shape
    xn = _rms(h, gpre_ref[...]).astype(BF16)

    i = pl.program_id(0)
    is_sample = i >= prompt_tiles
    z = z_ref[...]
    row = lax.broadcasted_iota(I32, z.shape, 0)
    t = row & (jnp.where(is_sample, ts, tm) - 1)
    keep = jnp.where(jnp.logical_or(is_sample, i % seq_tiles == 0), 0.0, 1.0)
    prev = zprev_ref[...] * keep
    p1, p2 = prev[SUBLANES - 1:SUBLANES], prev[SUBLANES - 2:SUBLANES - 1]
    r1 = jnp.where(is_sample, e1_ref[...], jnp.broadcast_to(p1, z.shape))
    r2 = jnp.where(is_sample, e2_ref[...], jnp.where(t == 0, p2, p1))
    z1 = jnp.where(t == 0, r1, pltpu.roll(z, 1, axis=0))
    z2 = jnp.where(t < 2, r2, pltpu.roll(z, 2, axis=0))
    cw = cw_ref[...]
    conv = cw[0:1] * z2 + cw[1:2] * z1 + cw[2:3] * z

    branches = ((od_ref[...], wbd_ref), (zb_ref[...] * conv, wbc_ref), (of_ref[...], wbf_ref))
    merged = jnp.zeros_like(h)
    for n, (val, w_ref) in enumerate(branches):
        gate = jax.nn.sigmoid(_dot(xn, wg_ref[:, n * d:(n + 1) * d]))
        merged = merged + gate * _dot(val.astype(BF16), w_ref[...])
    o_ref[...] = h + _rms(_dot(merged.astype(BF16), wo_ref[...]), gpost_ref[...])


def _merge(h, od, of, zb, z, e1, e2, g_pre, g_post, conv_w, wg, wbd, wbc, wbf, wo, l, n_p, seq, ts, *, tm):
    n, d = h.shape
    assert seq % tm == 0 and tm % ts == 0 and tm & (tm - 1) == 0 and ts & (ts - 1) == 0 and CONV_K == 3
    p_tiles = n_p // tm
    tok = lambda w: pl.BlockSpec((tm, w), lambda i: (i, 0))
    lay = lambda a, b: pl.BlockSpec((None, a, b), lambda i: (l, 0, 0))
    halo = pl.BlockSpec((SUBLANES, CONV_WIDTH), lambda i: (jnp.maximum(i * (tm // SUBLANES) - 1, 0), 0))
    smp = pl.BlockSpec((tm, CONV_WIDTH), lambda i: (jnp.maximum(i - p_tiles, 0), 0))
    return pl.pallas_call(
        functools.partial(_merge_kernel, prompt_tiles=p_tiles, seq_tiles=seq // tm, ts=ts),
        grid=(n // tm,),
        in_specs=[tok(d), tok(Q_WIDTH), tok(Q_WIDTH), tok(CONV_WIDTH), tok(CONV_WIDTH), halo, smp, smp,
                  lay(1, d), lay(1, d), lay(CONV_K, CONV_WIDTH),
                  lay(d, N_BRANCHES * d), lay(Q_WIDTH, d), lay(CONV_WIDTH, d), lay(Q_WIDTH, d), lay(d, d)],
        out_specs=tok(d),
        out_shape=jax.ShapeDtypeStruct((n, d), F32),
        compiler_params=_params("parallel"),
        name="merge",
    )(h, od, of, zb, z, z, e1, e2, g_pre, g_post, conv_w, wg, wbd, wbc, wbf, wo)


def _rope_tables(pos):
    half = ROPE_DIM // 2
    inv = jnp.power(jnp.float32(ROPE_THETA), -jnp.arange(half, dtype=jnp.float32) * (2.0 / ROPE_DIM))
    ang = pos.astype(jnp.float32)[:, None] * inv[None, :]
    cos, sin = jnp.cos(ang), jnp.sin(ang)
    n = pos.shape[0]
    ones = jnp.ones((n, HEAD_DIM - ROPE_DIM), F32)
    zeros_r = jnp.zeros((n, HEAD_DIM - ROPE_DIM), F32)
    zeros_h = jnp.zeros((n, half), F32)
    cos_t = jnp.concatenate([cos, cos, ones], axis=1)
    sa_t = jnp.concatenate([-sin, zeros_h, zeros_r], axis=1)
    sb_t = jnp.concatenate([zeros_h, sin, zeros_r], axis=1)
    rep = LANES // HEAD_DIM
    return tuple(jnp.tile(x, (1, rep)) for x in (cos_t, sa_t, sb_t))


def _pick(n, prefs):
    for p in prefs:
        if n % p == 0:
            return p
    return n


def kernel(x_prompt, x_sample, cache_dsa_k, cache_dsa_v, cache_idx_k, cache_fox_k, cache_fox_v, cache_fox_logf,
           state_conv, page_table, g_ffn1_pre, g_ffn1_post, w_ffn1_gu, w_ffn1_dn, g_mix_pre, g_mix_post, w_in,
           b_forget, conv_w, w_br_dsa, w_br_conv, w_br_fox, w_out, g_ffn2_pre, g_ffn2_post, w_ffn2_gu, w_ffn2_dn):
    bsz, seq, d = x_prompt.shape
    r, ts, _ = x_sample.shape
    depth = w_in.shape[0]
    n_pool = cache_dsa_k.shape[1]
    n_pages = page_table.shape[1]
    past = n_pages * PAGE_SIZE
    n_p, n_s = bsz * seq, r * ts
    dff = w_ffn1_dn.shape[1]
    assert ts == SUBLANES and seq % LANES == 0

    tm = _pick(math.gcd(n_p, n_s), (512, 256, 128, 64, 32, 16, 8))
    tf = _pick(dff, (1408, 1024, 512, 256, 128))
    tq = _pick(seq, (128,))
    tk = _pick(seq, (512, 256, 128))
    nk = seq // tk

    sizes = (Q_WIDTH, KV_WIDTH, KV_WIDTH, IDX_HEADS * IDX_DIM, IDX_DIM, IDX_HEADS, CONV_WIDTH, CONV_WIDTH,
             CONV_WIDTH, Q_WIDTH, KV_WIDTH, KV_WIDTH, FOX_HEADS, N_BRANCHES * d)
    offs = [0]
    for s_ in sizes:
        offs.append(offs[-1] + s_)
    col = lambda i: w_in[:, :, offs[i]:offs[i + 1]]
    (c_qd, c_kd, c_vd, c_qi, c_ki, c_wi, c_zb, c_zc, c_zx, c_qf, c_kf, c_vf, c_fl, c_g) = [col(i) for i in range(14)]
    ki_pad = jnp.concatenate([c_ki, jnp.zeros((depth, d, LANES - IDX_DIM), F32)], axis=2)
    w_rope = jnp.concatenate([c_qd, c_kd, c_qi, ki_pad], axis=2).astype(BF16)
    w_plain = jnp.concatenate([c_vd, c_zb, c_qf, c_kf, c_vf], axis=2).astype(BF16)
    w_z = jnp.concatenate([c_zc, c_zx], axis=2).astype(BF16)
    w_wi, w_fl, w_g = c_wi.astype(BF16), c_fl.astype(BF16), c_g.astype(BF16)
    w1gu, w1dn = w_ffn1_gu.astype(BF16), w_ffn1_dn.astype(BF16)
    w2gu, w2dn = w_ffn2_gu.astype(BF16), w_ffn2_dn.astype(BF16)
    wbd, wbc, wbf, wo = (w.astype(BF16) for w in (w_br_dsa, w_br_conv, w_br_fox, w_out))
    row = lambda g: g.reshape(depth, 1, -1)
    g1pre, g1post, gmpre, gmpost, g2pre, g2post, bfg = (
        row(g) for g in (g_ffn1_pre, g_ffn1_post, g_mix_pre, g_mix_post, g_ffn2_pre, g_ffn2_post, b_forget))

    pos = jnp.concatenate([jnp.tile(jnp.arange(seq, dtype=I32), bsz),
                           jnp.tile(past + jnp.arange(ts, dtype=I32), r)])
    cos_t, sa_t, sb_t = _rope_tables(pos)

    page_t = lambda c: jnp.moveaxis(c, 2, -1).reshape(depth, n_pool, -1, PAGE_SIZE)
    caches_t = [page_t(c) for c in (cache_dsa_k, cache_dsa_v, cache_idx_k, cache_fox_k, cache_fox_v, cache_fox_logf)]

    def chunks_t(a, w):
        return a.reshape(bsz, nk, tk, w).transpose(0, 1, 3, 2)

    x = jnp.concatenate([x_prompt.reshape(n_p, d), x_sample.reshape(n_s, d)], axis=0)
    stacks, tails_p, tails_s = None, [], []
    for l in range(depth):
        h = _ffn(x, g1pre, g1post, w1gu, w1dn, l, tm=tm, tf=tf)
        (qd, qi, zb, qf, z, wi), (kd_l, kf_l), stacks = _inproj(
            h, gmpre, cos_t, sa_t, sb_t, bfg, w_rope, w_plain, w_z, w_wi, w_fl, l, n_p, stacks, tm=tm)
        kd_p, vd_p, ki_p, kf_p, vf_p, lf_p, kd_s, vd_s, ki_s, kf_s, vf_s, lf_s = stacks

        o_dsa = _dsa_prompt(qd, qi.T, wi.T, chunks_t(kd_l, KV_WIDTH), vd_p, ki_p, bsz, seq, l, tq=tq, tk=tk)
        lf_blocks = lf_p[l].reshape(bsz, seq // LANES, LANES, FOX_HEADS).transpose(0, 1, 3, 2)
        f_k = _cumsum_prompt(lf_blocks.reshape(bsz, seq // LANES * FOX_HEADS, LANES))
        f_q = jnp.swapaxes(f_k, 1, 2)
        f_kc = f_k.reshape(bsz, FOX_HEADS, nk, tk).transpose(0, 2, 1, 3)
        o_fox = _fox_prompt(qf, chunks_t(kf_l, KV_WIDTH), vf_p, f_q, f_kc, bsz, seq, l, tq=tq, tk=tk)
        lf_new_t = jnp.pad(jnp.swapaxes(lf_s[l].reshape(r, ts, FOX_HEADS), 1, 2), ((0, 0), (0, 0), (0, LANES - ts)))
        o_dsa, o_fox = _sample_attn(page_table, caches_t, qd, qi, wi, kd_s, vd_s, ki_s, qf, kf_s, vf_s, lf_new_t,
                                    o_dsa, o_fox, n_p, l)

        st = state_conv[l].astype(F32)
        pad_t = lambda a: jnp.pad(a, ((0, 0), (0, ts - a.shape[1]), (0, 0))).reshape(n_s, CONV_WIDTH)
        h = _merge(h, o_dsa, o_fox, zb, z, pad_t(st[:, 1:]), pad_t(st), gmpre, gmpost, conv_w,
                   w_g, wbd, wbc, wbf, wo, l, n_p, seq, ts, tm=tm)
        x = _ffn(h, g2pre, g2post, w2gu, w2dn, l, tm=tm, tf=tf)

        tails_p.append(z[:n_p].reshape(bsz, seq, CONV_WIDTH)[:, seq - (CONV_K - 1):])
        tails_s.append(z[n_p:].reshape(r, ts, CONV_WIDTH)[:, ts - (CONV_K - 1):])

    def rows(arrs, lead):
        kd_, vd_, ki_, kf_, vf_, lf_ = arrs
        heads = lambda a: a.reshape(depth, *lead, DSA_KV_HEADS, HEAD_DIM)
        return [heads(kd_), heads(vd_), ki_.reshape(depth, *lead, IDX_DIM), heads(kf_), heads(vf_),
                lf_.reshape(depth, *lead, FOX_HEADS)]

    sp = rows(stacks[:6], (bsz, seq)) + [jnp.stack(tails_p, axis=0)]
    ss = rows(stacks[6:], (r, ts)) + [jnp.stack(tails_s, axis=0)]
    return (x[:n_p].reshape(bsz, seq, d), x[n_p:].reshape(r, ts, d),
            sp[0], sp[1], sp[2], sp[3], sp[4], sp[5], sp[6],
            ss[0], ss[1], ss[2], ss[3], ss[4], ss[5], ss[6])
```

```python
import functools
import math

import jax
import jax.numpy as jnp
from jax import lax
from jax.experimental import pallas as pl
from jax.experimental.pallas import tpu as pltpu

HEAD_DIM = 64
DSA_HEADS = 8
DSA_KV_HEADS = 4
IDX_HEADS = 8
IDX_DIM = 64
TOPK_MAX = 256
CONV_WIDTH = 512
CONV_K = 3
FOX_HEADS = 8
FOX_KV_HEADS = 4
ROPE_THETA = 500000.0
ROPE_DIM = HEAD_DIM // 4
NORM_EPS = 1e-6
N_BRANCHES = 3
PAGE_SIZE = 128

LANES = 128
SUBLANES = 8
KV_WIDTH = DSA_KV_HEADS * HEAD_DIM
Q_WIDTH = DSA_HEADS * HEAD_DIM
VMEM_LIMIT = 56 * 1024 * 1024

F32 = jnp.float32
BF16 = jnp.bfloat16
I32 = jnp.int32
NEG = -1e30
INT_MIN = -(2 ** 31)
NEG_INF_KEY = INT_MIN + 0x7FFFFF
SEARCH_STAGES = (0, 24, 28, 32)

assert DSA_HEADS == FOX_HEADS and DSA_KV_HEADS == FOX_KV_HEADS
assert DSA_HEADS // DSA_KV_HEADS == 2 and 2 * HEAD_DIM == LANES


def _rms(x, g):
    return x * lax.rsqrt(jnp.mean(x * x, axis=-1, keepdims=True) + NORM_EPS) * g


def _nt_dot(a, b):
    return lax.dot_general(a, b, (((1,), (1,)), ((), ())), preferred_element_type=F32)


def _dot(a, b):
    return jnp.dot(a, b, preferred_element_type=F32)


def _params(*sem):
    return pltpu.CompilerParams(dimension_semantics=sem, vmem_limit_bytes=VMEM_LIMIT)


def _ffn_kernel(x_ref, gpre_ref, gpost_ref, wa_ref, wb_ref, wd_ref, o_ref, xn_ref, acc_ref, *, nf):
    j = pl.program_id(1)

    @pl.when(j == 0)
    def _():
        xn_ref[...] = _rms(x_ref[...], gpre_ref[...]).astype(BF16)
        acc_ref[...] = jnp.zeros_like(acc_ref)

    xn = xn_ref[...]
    a = _dot(xn, wa_ref[...])
    b = _dot(xn, wb_ref[...])
    act = (a * jax.nn.sigmoid(a) * b).astype(BF16)
    acc_ref[...] += _dot(act, wd_ref[...])

    @pl.when(j == nf - 1)
    def _():
        o_ref[...] = x_ref[...] + 0.5 * _rms(acc_ref[...], gpost_ref[...])


def _ffn(x, g_pre, g_post, w_gu, w_dn, l, *, tm, tf):
    n, d = x.shape
    dff = w_dn.shape[1]
    nf = dff // tf
    return pl.pallas_call(
        functools.partial(_ffn_kernel, nf=nf),
        grid=(n // tm, nf),
        in_specs=[
            pl.BlockSpec((tm, d), lambda i, j: (i, 0)),
            pl.BlockSpec((None, 1, d), lambda i, j: (l, 0, 0)),
            pl.BlockSpec((None, 1, d), lambda i, j: (l, 0, 0)),
            pl.BlockSpec((None, d, tf), lambda i, j: (l, 0, j)),
            pl.BlockSpec((None, d, tf), lambda i, j: (l, 0, j + nf)),
            pl.BlockSpec((None, tf, d), lambda i, j: (l, j, 0)),
        ],
        out_specs=pl.BlockSpec((tm, d), lambda i, j: (i, 0)),
        out_shape=jax.ShapeDtypeStruct((n, d), F32),
        scratch_shapes=[pltpu.VMEM((tm, d), BF16), pltpu.VMEM((tm, d), F32)],
        compiler_params=_params("parallel", "arbitrary"),
        name="ffn",
    )(x, g_pre, g_post, w_gu, w_gu, w_dn)


ROPE_COLS = Q_WIDTH + KV_WIDTH + Q_WIDTH + LANES
PLAIN_OFFSETS = (0, KV_WIDTH, KV_WIDTH + CONV_WIDTH, KV_WIDTH + CONV_WIDTH + Q_WIDTH,
                 2 * KV_WIDTH + CONV_WIDTH + Q_WIDTH, 3 * KV_WIDTH + CONV_WIDTH + Q_WIDTH)
PLAIN_COLS = PLAIN_OFFSETS[-1]


def _log_sigmoid(x):
    return jnp.minimum(x, 0.0) - jnp.log1p(jnp.exp(-jnp.abs(x)))


def _inproj_kernel(x_ref, g_ref, cos_ref, sa_ref, sb_ref, bf_ref, wr_ref, wp_ref, wz_ref, wwi_ref, wfl_ref,
                   *refs, prompt_tiles):
    (qd_o, qi_o, zb_o, qf_o, z_o, wi_o, vd_l, ki_l, vf_l, kd_t, vd_t, ki_t, kf_t, vf_t, lf_p,
     kd_s, vd_s, ki_s, kf_s, vf_s, lf_s) = refs[-21:]
    xn = _rms(x_ref[...], g_ref[...]).astype(BF16)
    cos, sa, sb = cos_ref[...], sa_ref[...], sb_ref[...]

    r = _dot(xn, wr_ref[...])
    outs = []
    for c in range(ROPE_COLS // LANES):
        v = r[:, c * LANES:(c + 1) * LANES]
        up = pltpu.roll(v, LANES - ROPE_DIM // 2, axis=1)
        dn = pltpu.roll(v, ROPE_DIM // 2, axis=1)
        outs.append(v * cos + up * sa + dn * sb)
    nq = Q_WIDTH // LANES
    nk = KV_WIDTH // LANES
    qd_o[...] = jnp.concatenate(outs[:nq], axis=1)
    kd = jnp.concatenate(outs[nq:nq + nk], axis=1)
    qi_o[...] = jnp.concatenate(outs[nq + nk:2 * nq + nk], axis=1)
    ki_pad = outs[2 * nq + nk]
    ki = ki_pad[:, :IDX_DIM]

    p = _dot(xn, wp_ref[...])
    vd, zb, qf, kf, vf = (p[:, a:b] for a, b in zip(PLAIN_OFFSETS[:-1], PLAIN_OFFSETS[1:]))
    zb_o[...] = zb
    qf_o[...] = qf

    zz = _dot(xn, wz_ref[...])
    z_o[...] = zz[:, :CONV_WIDTH] * zz[:, CONV_WIDTH:]

    wi_o[...] = _dot(xn, wwi_ref[...]) * (IDX_HEADS ** -0.5)
    lf = _log_sigmoid(_dot(xn, wfl_ref[...]) + bf_ref[...])

    is_prompt = pl.program_id(0) < prompt_tiles
    @pl.when(is_prompt)
    def _():
        for ref, val in ((vd_l, vd), (ki_l, ki), (vf_l, vf), (lf_p, lf), (kd_t, kd.T), (vd_t, vd.T),
                         (ki_t, ki_pad.T[:IDX_DIM]), (kf_t, kf.T), (vf_t, vf.T)):
            ref[...] = val

    @pl.when(jnp.logical_not(is_prompt))
    def _():
        for ref, val in zip((kd_s, vd_s, ki_s, kf_s, vf_s, lf_s), (kd, vd, ki, kf, vf, lf)):
            ref[...] = val


def _inproj(x, g, cos, sa, sb, b_forget, wr, wp, wz, wwi, wfl, l, bsz, seq, stacks, *, tm):
    n, d = x.shape
    depth = wr.shape[0]
    n_p = bsz * seq
    p_tiles, s_tiles, tps = n_p // tm, (n - n_p) // tm, seq // tm
    pi = lambda i: jnp.minimum(i, p_tiles - 1)
    tok = lambda w: pl.BlockSpec((tm, w), lambda i: (i, 0))
    tok_l = lambda w: pl.BlockSpec((tm, w), lambda i: (pi(i), 0))
    tok_t = lambda w: pl.BlockSpec((None, None, w, tm), lambda i: (l, pi(i) // tps, 0, pi(i) % tps))
    tok_p = lambda w: pl.BlockSpec((None, tm, w), lambda i: (l, pi(i), 0))
    tok_s = lambda w: pl.BlockSpec((None, tm, w), lambda i: (l, jnp.maximum(i - p_tiles, 0), 0))
    lay = lambda a, b: pl.BlockSpec((None, a, b), lambda i: (l, 0, 0))
    all_w = (Q_WIDTH, Q_WIDTH, CONV_WIDTH, Q_WIDTH, CONV_WIDTH, IDX_HEADS)
    lay_w = (KV_WIDTH, IDX_DIM, KV_WIDTH)
    t_w = (KV_WIDTH, KV_WIDTH, IDX_DIM, KV_WIDTH, KV_WIDTH)
    row_w = (KV_WIDTH, KV_WIDTH, IDX_DIM, KV_WIDTH, KV_WIDTH, FOX_HEADS)
    in_specs = [tok(d), lay(1, d), tok(LANES), tok(LANES), tok(LANES), lay(1, FOX_HEADS),
                lay(d, ROPE_COLS), lay(d, PLAIN_COLS), lay(d, 2 * CONV_WIDTH), lay(d, IDX_HEADS), lay(d, FOX_HEADS)]
    args = [x, g, cos, sa, sb, b_forget, wr, wp, wz, wwi, wfl]
    n_plain = len(all_w) + len(lay_w)
    aliases = {}
    if stacks is not None:
        aliases = {len(args) + k: n_plain + k for k in range(len(stacks))}
        in_specs = in_specs + [pl.BlockSpec(memory_space=pl.ANY)] * len(stacks)
        args = args + list(stacks)
    sds = jax.ShapeDtypeStruct
    outs = pl.pallas_call(
        functools.partial(_inproj_kernel, prompt_tiles=p_tiles),
        grid=(n // tm,),
        in_specs=in_specs,
        out_specs=([tok(w) for w in all_w] + [tok_l(w) for w in lay_w] + [tok_t(w) for w in t_w]
                   + [tok_p(FOX_HEADS)] + [tok_s(w) for w in row_w]),
        out_shape=([sds((n, w), F32) for w in all_w] + [sds((n_p, w), F32) for w in lay_w]
                   + [sds((depth, bsz, w, seq), F32) for w in t_w] + [sds((depth, n_p, FOX_HEADS), F32)]
                   + [sds((depth, s_tiles * tm, w), F32) for w in row_w]),
        input_output_aliases=aliases,
        compiler_params=_params("arbitrary"),
        name="inproj",
    )(*args)
    return outs[:6], outs[6:n_plain], outs[n_plain:]


def _block_cumsum(x, n_blocks):
    rows = n_blocks * SUBLANES
    padded = -(-rows // LANES) * LANES
    if padded != rows:
        x = jnp.concatenate([x, jnp.zeros((padded - rows, LANES), F32)], axis=0)
    hi = functools.partial(jnp.dot, preferred_element_type=F32, precision=lax.Precision.HIGHEST)
    r0 = lax.broadcasted_iota(I32, (LANES, LANES), 0)
    c0 = lax.broadcasted_iota(I32, (LANES, LANES), 1)
    within = hi(x, jnp.where(r0 <= c0, 1.0, 0.0).astype(F32))
    totals = jnp.broadcast_to(within[:, LANES - 1:LANES], (padded, LANES))
    dist = lax.broadcasted_iota(I32, (padded, padded), 0) - lax.broadcasted_iota(I32, (padded, padded), 1)
    earlier = jnp.where((dist > 0) & ((dist & (SUBLANES - 1)) == 0), 1.0, 0.0).astype(F32)
    full = within + hi(earlier, totals)
    return [full[b * SUBLANES:(b + 1) * SUBLANES] for b in range(n_blocks)]


def _cumsum_kernel(x_ref, o_ref, *, n_blocks):
    o_ref[...] = jnp.concatenate(_block_cumsum(x_ref[...], n_blocks), axis=1)


def _cumsum_prompt(lf_blocks):
    b, rows, _ = lf_blocks.shape
    n_blocks = rows // FOX_HEADS
    t = n_blocks * LANES
    return pl.pallas_call(
        functools.partial(_cumsum_kernel, n_blocks=n_blocks),
        grid=(b,),
        in_specs=[pl.BlockSpec((None, rows, LANES), lambda i: (i, 0, 0))],
        out_specs=pl.BlockSpec((None, FOX_HEADS, t), lambda i: (i, 0, 0)),
        out_shape=jax.ShapeDtypeStruct((b, FOX_HEADS, t), F32),
        compiler_params=_params("parallel"),
        name="cumsum_prompt",
    )(lf_blocks)


def _q_block_diag(q):
    t = q.shape[0]
    lane = lax.broadcasted_iota(I32, (t, LANES), 1)
    zeros = jnp.zeros((t, LANES), F32)
    blocks = []
    for h in range(DSA_HEADS):
        g, r = divmod(h, 2)
        src = q[:, g * LANES:(g + 1) * LANES]
        if r != g % 2:
            src = pltpu.roll(src, HEAD_DIM, axis=1)
        keep = (lane < HEAD_DIM) if g % 2 == 0 else (lane >= HEAD_DIM)
        m = jnp.where(keep, src, 0.0)
        blocks.append(jnp.concatenate([m, zeros] if g // 2 == 0 else [zeros, m], axis=1))
    return jnp.concatenate(blocks, axis=0).astype(BF16)


def _head_rows(q):
    return jnp.concatenate([q[:, h * IDX_DIM:(h + 1) * IDX_DIM] for h in range(IDX_HEADS)], axis=0)


def _extract_heads(acc, t):
    outs = []
    for h in range(DSA_HEADS):
        g = h // 2
        outs.append(acc[h * t:(h + 1) * t, g * HEAD_DIM:(g + 1) * HEAD_DIM])
    return jnp.concatenate(outs, axis=1)


def _sort_key(score):
    bits = pltpu.bitcast(score, I32)
    return jnp.where(bits < 0, bits ^ 0x7FFFFFFF, bits)


def _indexer_score(s, wi, t):
    score = jnp.zeros((t, s.shape[1]), F32)
    for h in range(s.shape[0] // t):
        score = score + jnp.maximum(s[h * t:(h + 1) * t], 0.0) * wi[:, h:h + 1]
    return jnp.where(score == 0.0, 0.0, score)


def _fold_lanes(x):
    out = x[:, :LANES]
    for j in range(1, x.shape[1] // LANES):
        out = out + x[:, j * LANES:(j + 1) * LANES]
    return out


LOG2E = 1.4426950408889634


def _two_pass_attention(qbd, kt_ref, v_ref, loops, bias_fn, s_ref, mx_ref, l_ref, acc_ref, tq, tk):
    group = DSA_HEADS // DSA_KV_HEADS
    n_tiles = tk // LANES

    def fold(x, op):
        out = x[:, :, :LANES]
        for j in range(1, n_tiles):
            out = op(out, x[:, :, j * LANES:(j + 1) * LANES])
        return out

    def logit_chunk(c, carry, *, tag):
        kt = kt_ref[:, pl.ds(pl.multiple_of(c * tk, tk), tk)].astype(BF16)
        bias = bias_fn(c, tag)
        for g in range(DSA_HEADS // group):
            h0, h1 = g * group, (g + 1) * group
            s3 = _dot(qbd[h0 * tq:h1 * tq], kt).reshape(group, tq, tk) + bias(h0, h1)
            s_ref[c, h0:h1] = s3
            mx_ref[h0:h1] = jnp.maximum(mx_ref[h0:h1], fold(s3, jnp.maximum))
        return carry

    def pv_chunk(c, carry):
        v = v_ref[pl.ds(pl.multiple_of(c * tk, tk), tk), :].astype(BF16)
        for g in range(DSA_HEADS // group):
            h0, h1 = g * group, (g + 1) * group
            m = jnp.concatenate([mx_ref[h0:h1]] * n_tiles, axis=2)
            p = jnp.exp2(s_ref[c, h0:h1] - m)
            l_ref[h0:h1] += fold(p, jnp.add)
            acc_ref[h0:h1] += _dot(p.reshape(group * tq, tk).astype(BF16), v).reshape(group, tq, KV_WIDTH)
        return carry

    mx_ref[...] = jnp.full_like(mx_ref, NEG)
    for lo, hi, tag in loops:
        lax.fori_loop(lo, hi, functools.partial(logit_chunk, tag=tag), 0)
    mx_ref[...] = jnp.broadcast_to(jnp.max(mx_ref[...], axis=2, keepdims=True), mx_ref.shape)
    l_ref[...] = jnp.zeros_like(l_ref)
    acc_ref[...] = jnp.zeros_like(acc_ref)
    lax.fori_loop(loops[0][0], loops[-1][1], pv_chunk, 0)
    out = acc_ref[...] / jnp.sum(l_ref[...], axis=2, keepdims=True)
    return _extract_heads(out.reshape(DSA_HEADS * tq, KV_WIDTH), tq)


def _two_pass_scratch(tq, tk, nk):
    return [pltpu.VMEM((nk, DSA_HEADS, tq, tk), F32), pltpu.VMEM((DSA_HEADS, tq, LANES), F32),
            pltpu.VMEM((DSA_HEADS, tq, LANES), F32), pltpu.VMEM((DSA_HEADS, tq, KV_WIDTH), F32)]


def _fox_prompt_kernel(q_ref, kt_ref, v_ref, fq_ref, fk_ref, o_ref, s_ref, m_ref, l_ref, acc_ref, *, tq, tk):
    i = pl.program_id(1)
    qbd = _q_block_diag(q_ref[...] * (HEAD_DIM ** -0.5 * LOG2E))
    fq = fq_ref[...] * LOG2E
    fq_rep = jnp.stack([jnp.broadcast_to(fq[:, h:h + 1], (tq, LANES)) for h in range(FOX_HEADS)], axis=0)
    qpos = i * tq + lax.broadcasted_iota(I32, (tq, tk), 0)
    kiota = lax.broadcasted_iota(I32, (tq, tk), 1)

    def bias(c, masked):
        fk = fk_ref[c] * LOG2E
        fk3 = jnp.stack([fk[h:h + 1, :] for h in range(FOX_HEADS)], axis=0)
        causal = ((kiota + c * tk) <= qpos)[None]

        def heads(h0, h1):
            b = jnp.concatenate([fq_rep[h0:h1]] * (tk // LANES), axis=2) - fk3[h0:h1]
            return jnp.where(causal, b, NEG) if masked else b

        return heads

    n_full = (i * tq + 1) // tk
    loops = ((0, n_full, False), (n_full, (i * tq + tq - 1) // tk + 1, True))
    o_ref[...] = _two_pass_attention(qbd, kt_ref, v_ref, loops, bias, s_ref, m_ref, l_ref, acc_ref, tq, tk)


def _fox_prompt(q, kt, v, fq, fk, bsz, seq, l, *, tq, tk):
    n = q.shape[0]
    nq, nk = seq // tq, seq // tk
    return pl.pallas_call(
        functools.partial(_fox_prompt_kernel, tq=tq, tk=tk),
        grid=(bsz, nq),
        in_specs=[
            pl.BlockSpec((tq, Q_WIDTH), lambda bi, i: (bi * nq + i, 0)),
            pl.BlockSpec((None, None, KV_WIDTH, seq), lambda bi, i: (l, bi, 0, 0)),
            pl.BlockSpec((seq, KV_WIDTH), lambda bi, i: (bi, 0)),
            pl.BlockSpec((None, tq, FOX_HEADS), lambda bi, i: (bi, i, 0)),
            pl.BlockSpec((None, nk, FOX_HEADS, tk), lambda bi, i: (bi, 0, 0, 0)),
        ],
        out_specs=pl.BlockSpec((tq, Q_WIDTH), lambda bi, i: (bi * nq + i, 0)),
        out_shape=jax.ShapeDtypeStruct((n, Q_WIDTH), F32),
        scratch_shapes=_two_pass_scratch(tq, tk, nk),
        compiler_params=_params("parallel", "parallel"),
        name="fox_prompt",
    )(q, kt, v, fq, fk)


def _dsa_prompt_kernel(q_ref, qit_ref, wit_ref, kt_ref, v_ref, ki_ref, o_ref,
                       key_ref, thr_ref, cnt_ref, j_ref, s_ref, m_ref, l_ref, acc_ref, *, tq, tk, top_k, seq):
    i = pl.program_id(1)
    n_vis = (i * tq + tq - 1) // tk + 1
    kpos = lax.broadcasted_iota(I32, (tk, tq), 0)
    qpos = i * tq + lax.broadcasted_iota(I32, (tk, tq), 1)
    kf = float(top_k)

    qit = qit_ref[...].astype(BF16)
    qi_t = jnp.concatenate([qit[h * IDX_DIM:(h + 1) * IDX_DIM] for h in range(IDX_HEADS)], axis=1)
    wit = wit_ref[...] * (IDX_DIM ** -0.5)

    def score_chunk(c, carry):
        start = pl.multiple_of(c * tk, tk)
        s = _dot(ki_ref[pl.ds(start, tk), :].astype(BF16), qi_t)
        score = jnp.zeros((tk, tq), F32)
        for h in range(IDX_HEADS):
            score = score + jnp.maximum(s[:, h * tq:(h + 1) * tq], 0.0) * wit[h:h + 1, :]
        score = jnp.where(score == 0.0, 0.0, score)
        score = jnp.where((kpos + c * tk) <= qpos, score, -jnp.inf)
        key_ref[c] = _sort_key(score)
        return carry

    lax.fori_loop(0, n_vis, score_chunk, 0)

    def partial_count(mask):
        ones = jnp.where(mask, 1.0, 0.0).reshape(8, tk // (8 * SUBLANES), SUBLANES, tq)
        c = [jnp.sum(ones[g], axis=0) for g in range(8)]
        return ((c[0] + c[1]) + (c[2] + c[3])) + ((c[4] + c[5]) + (c[6] + c[7]))

    def count(*preds):
        def body(c, cnts):
            key, pos = key_ref[c], kpos + c * tk
            return tuple(cnt + partial_count(pred(key, pos)) for cnt, pred in zip(cnts, preds))
        zero = jnp.zeros((SUBLANES, tq), F32)
        totals = lax.fori_loop(0, n_vis, body, tuple(zero for _ in preds))
        return tuple(jnp.sum(x, axis=0, keepdims=True) for x in totals)

    def search(n_chunks):
        def step(b, carry):
            cand, cnt_at = carry
            trial = cand ^ lax.shift_left(jnp.int32(1), 31 - b)
            cnt = jnp.zeros((SUBLANES, tq), F32)
            for c in range(n_chunks):
                cnt = cnt + partial_count(key_ref[c] >= trial)
            total = jnp.sum(cnt, axis=0, keepdims=True)
            ok = total >= kf
            return jnp.where(ok, trial, cand), jnp.where(ok, total, cnt_at)

        carry = (jnp.full((1, tq), INT_MIN, I32), jnp.full((1, tq), float(n_chunks * tk), F32))
        for lo, hi in zip(SEARCH_STAGES[:-1], SEARCH_STAGES[1:]):
            if lo == 0:
                carry = lax.fori_loop(lo, hi, step, carry)
                thr_ref[...], cnt_ref[...] = carry
            else:
                @pl.when(jnp.max(jnp.where(cnt_ref[...] != kf, 1.0, 0.0)) > 0.0)
                def _():
                    thr_ref[...], cnt_ref[...] = lax.fori_loop(lo, hi, step, (thr_ref[...], cnt_ref[...]))

    for n_chunks in range(1, seq // tk + 1):
        @pl.when(n_vis == n_chunks)
        def _():
            search(n_chunks)

    thr = thr_ref[...]

    n_gt, n_ge = count(lambda key, pos: key > thr, lambda key, pos: key >= thr)
    need = kf - n_gt
    j_ref[...] = jnp.full_like(j_ref, seq)
    tied = jnp.where((n_ge > kf) & (thr != NEG_INF_KEY), 1.0, 0.0)

    @pl.when(jnp.max(tied) > 0.0)
    def _():
        def idx_step(b, lo):
            trial = lo | lax.shift_left(jnp.int32(1), int(math.log2(seq)) - 1 - b)
            (c,) = count(lambda key, pos: jnp.where(key == thr, pos, seq) < trial)
            return jnp.where(c < need, trial, lo)

        j_ref[...] = lax.fori_loop(0, int(math.log2(seq)), idx_step, jnp.zeros((1, tq), I32))

    jmax = j_ref[...]

    qbd = _q_block_diag(q_ref[...] * (HEAD_DIM ** -0.5 * LOG2E))

    def bias(c, tag):
        key = key_ref[c]
        tie_bias = jnp.where(jnp.where(key == thr, kpos + c * tk, seq + 1) <= jmax, 0.0, NEG)
        b = jnp.where(key > NEG_INF_KEY, jnp.where(key > thr, 0.0, tie_bias), NEG).T[None]
        return lambda h0, h1: b

    o_ref[...] = _two_pass_attention(qbd, kt_ref, v_ref, ((0, n_vis, None),), bias, s_ref, m_ref, l_ref, acc_ref,
                                     tq, tk)


def _dsa_prompt(q, qit, wit, kt, v, ki, bsz, seq, l, *, tq, tk):
    n = q.shape[0]
    nq, nk = seq // tq, seq // tk
    top_k = min(TOPK_MAX, seq // 4)
    return pl.pallas_call(
        functools.partial(_dsa_prompt_kernel, tq=tq, tk=tk, top_k=top_k, seq=seq),
        grid=(bsz, nq),
        in_specs=[
            pl.BlockSpec((tq, Q_WIDTH), lambda bi, i: (bi * nq + i, 0)),
            pl.BlockSpec((Q_WIDTH, tq), lambda bi, i: (0, bi * nq + i)),
            pl.BlockSpec((IDX_HEADS, tq), lambda bi, i: (0, bi * nq + i)),
            pl.BlockSpec((None, None, KV_WIDTH, seq), lambda bi, i: (l, bi, 0, 0)),
            pl.BlockSpec((seq, KV_WIDTH), lambda bi, i: (bi, 0)),
            pl.BlockSpec((seq, IDX_DIM), lambda bi, i: (bi, 0)),
        ],
        out_specs=pl.BlockSpec((tq, Q_WIDTH), lambda bi, i: (bi * nq + i, 0)),
        out_shape=jax.ShapeDtypeStruct((n, Q_WIDTH), F32),
        scratch_shapes=[pltpu.VMEM((nk, tk, tq), I32), pltpu.VMEM((1, tq), I32), pltpu.VMEM((1, tq), F32),
                        pltpu.VMEM((1, tq), I32)]
        + _two_pass_scratch(tq, tk, nk),
        compiler_params=_params("parallel", "parallel"),
        name="dsa_prompt",
    )(q, qit, wit, kt, v, ki)


def _page_specs(n_pages, rows, l):
    return [pl.BlockSpec((None, None, rows, PAGE_SIZE),
                         functools.partial(lambda b, pt, p: (l, pt[b * n_pages + p], 0, 0), p=p))
            for p in range(n_pages)]


def _pad_rows(x, rows):
    return jnp.concatenate([x, jnp.zeros((rows - x.shape[0], x.shape[1]), x.dtype)], axis=0)


def _paged_logits(lhs, pages_t, new_rows):
    blocks = [_dot(lhs, p[...].astype(BF16)) for p in pages_t]
    blocks.append(_nt_dot(lhs, _pad_rows(new_rows, PAGE_SIZE).astype(BF16)))
    return jnp.concatenate(blocks, axis=1)


def _softmax_pv(logits, v_pages_t, v_new, t):
    m = jnp.max(logits, axis=1, keepdims=True)
    p = jnp.exp(logits - m)
    l = jnp.sum(p, axis=1, keepdims=True)
    pb = p.astype(BF16)
    acc = jnp.zeros((logits.shape[0], KV_WIDTH), F32)
    for n, vp in enumerate(v_pages_t):
        acc = acc + _nt_dot(pb[:, n * PAGE_SIZE:(n + 1) * PAGE_SIZE], vp[...].astype(BF16))
    n = len(v_pages_t)
    acc = acc + _dot(pb[:, n * PAGE_SIZE:(n + 1) * PAGE_SIZE], _pad_rows(v_new, PAGE_SIZE).astype(BF16))
    return _extract_heads(acc / l, t)


def _visible(t, length, past):
    pos = lax.broadcasted_iota(I32, (t, length), 1)
    return pos, pos <= past + lax.broadcasted_iota(I32, (t, length), 0)


def _fox_sample_body(kp, vp, lp, q_ref, kn_ref, vn_ref, lfn_ref, t):
    n_pages = len(kp)
    past = n_pages * PAGE_SIZE
    length = past + PAGE_SIZE

    f_tiles = _block_cumsum(jnp.concatenate([p[...] for p in lp] + [lfn_ref[...]], axis=0), n_pages + 1)
    fk = jnp.concatenate(f_tiles, axis=1)
    lane = lax.broadcasted_iota(I32, (t, LANES), 1)
    diag = lane == lax.broadcasted_iota(I32, (t, LANES), 0)

    qbd = _q_block_diag(q_ref[...] * (HEAD_DIM ** -0.5))
    s = _paged_logits(qbd, kp, kn_ref[...])
    _, visible = _visible(t, length, past)
    rows = []
    for h in range(FOX_HEADS):
        f_new = jnp.broadcast_to(f_tiles[n_pages][h:h + 1, :], (t, LANES))
        fq = jnp.sum(jnp.where(diag, f_new, 0.0), axis=1, keepdims=True)
        sh = s[h * t:(h + 1) * t] + (fq - fk[h:h + 1, :])
        rows.append(jnp.where(visible, sh, NEG))
    return _softmax_pv(jnp.concatenate(rows, axis=0), vp, vn_ref[...], t)


def _sample_topk(key, pos, top_k, length, j_ref):
    t = key.shape[0]
    kf = float(top_k)
    count = lambda mask: jnp.sum(jnp.where(mask, 1.0, 0.0), axis=1, keepdims=True)

    def digit_pass(cand, shift, n_trials):
        digit = jnp.zeros((t, 1), I32)
        for j in range(1, n_trials + 1):
            trial = cand + jnp.left_shift(jnp.int32(j), shift)
            digit = digit + jnp.where(count(key >= trial) >= kf, 1, 0)
        return cand + jnp.left_shift(digit, shift)

    thr = digit_pass(jnp.full((t, 1), INT_MIN, I32), 30, 3)
    for shift in range(27, -1, -3):
        thr = digit_pass(thr, shift, 7)

    need = kf - count(key > thr)
    n_ge = count(key >= thr)
    j_ref[...] = jnp.full_like(j_ref, length)
    tied = jnp.where((n_ge > kf) & (thr != NEG_INF_KEY), 1.0, 0.0)

    @pl.when(jnp.max(tied) > 0.0)
    def _():
        n_bits = int(math.ceil(math.log2(length)))
        tie_pos = jnp.where(key == thr, pos, length)

        def idx_step(b, lo):
            trial = lo | lax.shift_left(jnp.int32(1), n_bits - 1 - b)
            return jnp.where(count(tie_pos < trial) < need, trial, lo)

        j_ref[...] = lax.fori_loop(0, n_bits, idx_step, jnp.zeros((t, 1), I32))

    return thr, j_ref[...]


def _dsa_sample_body(kp, vp, ip, q_ref, qi_ref, wi_ref, kn_ref, vn_ref, in_ref, j_ref, t, top_k):
    n_pages = len(kp)
    past = n_pages * PAGE_SIZE
    length = past + PAGE_SIZE

    qi = _head_rows(qi_ref[...]).astype(BF16)
    score = _indexer_score(_paged_logits(qi, ip, in_ref[...]), wi_ref[...] * (IDX_DIM ** -0.5), t)
    pos, visible = _visible(t, length, past)
    key = _sort_key(jnp.where(visible, score, -jnp.inf))
    thr, jmax = _sample_topk(key, pos, top_k, length, j_ref)
    tie_bias = jnp.where(jnp.where(key == thr, pos, length + 1) <= jmax, 0.0, NEG)
    bias = jnp.where(key > NEG_INF_KEY, jnp.where(key > thr, 0.0, tie_bias), NEG)

    qbd = _q_block_diag(q_ref[...] * (HEAD_DIM ** -0.5))
    logits = _paged_logits(qbd, kp, kn_ref[...]) + jnp.concatenate([bias] * DSA_HEADS, axis=0)
    return _softmax_pv(logits, vp, vn_ref[...], t)


def _sample_attn_kernel(pt_ref, *refs, n_pages, t, top_k):
    pages = [refs[n * n_pages:(n + 1) * n_pages] for n in range(6)]
    kd, vd, ki, kf, vf, lf = pages
    (qd_ref, qi_ref, wi_ref, kdn_ref, vdn_ref, kin_ref, qf_ref, kfn_ref, vfn_ref, lfn_ref, _, _,
     od_ref, of_ref, j_ref) = refs[6 * n_pages:]
    of_ref[...] = _fox_sample_body(kf, vf, lf, qf_ref, kfn_ref, vfn_ref, lfn_ref, t)
    od_ref[...] = _dsa_sample_body(kd, vd, ki, qd_ref, qi_ref, wi_ref, kdn_ref, vdn_ref, kin_ref, j_ref, t, top_k)


def _sample_attn(page_table, caches_t, qd, qi, wi, kd, vd, ki, qf, kf, vf, lf_new_t, o_dsa, o_fox, n_p, l):
    r, n_pages = page_table.shape
    t = (qd.shape[0] - n_p) // r
    off = n_p // t
    top_k = min(TOPK_MAX, (n_pages * PAGE_SIZE + t) // 4)
    req = lambda w: pl.BlockSpec((t, w), lambda bi, pt: (off + bi, 0))
    page_in = []
    for c in caches_t:
        page_in += _page_specs(n_pages, c.shape[2], l)
    new = lambda w: pl.BlockSpec((None, t, w), lambda bi, pt: (l, bi, 0))
    row_specs = [req(Q_WIDTH), req(Q_WIDTH), req(IDX_HEADS), new(KV_WIDTH), new(KV_WIDTH), new(IDX_DIM),
                 req(Q_WIDTH), new(KV_WIDTH), new(KV_WIDTH)]
    grid_spec = pltpu.PrefetchScalarGridSpec(
        num_scalar_prefetch=1,
        grid=(r,),
        in_specs=(page_in + row_specs
                  + [pl.BlockSpec((None, FOX_HEADS, LANES), lambda bi, pt: (bi, 0, 0)),
                     pl.BlockSpec(memory_space=pl.ANY), pl.BlockSpec(memory_space=pl.ANY)]),
        out_specs=[req(Q_WIDTH), req(Q_WIDTH)],
        scratch_shapes=[pltpu.VMEM((t, 1), I32)],
    )
    n_in = 1 + 6 * n_pages + len(row_specs) + 3
    page_args = [c for c in caches_t for _ in range(n_pages)]
    return pl.pallas_call(
        functools.partial(_sample_attn_kernel, n_pages=n_pages, t=t, top_k=top_k),
        grid_spec=grid_spec,
        out_shape=[jax.ShapeDtypeStruct(o_dsa.shape, F32), jax.ShapeDtypeStruct(o_fox.shape, F32)],
        input_output_aliases={n_in - 2: 0, n_in - 1: 1},
        compiler_params=_params("parallel"),
        name="sample_attn",
    )(page_table.reshape(-1), *page_args, qd, qi, wi, kd, vd, ki, qf, kf, vf, lf_new_t, o_dsa, o_fox)


def _merge_kernel(h_ref, od_ref, of_ref, zb_ref, z_ref, zprev_ref, e1_ref, e2_ref, gpre_ref, gpost_ref, cw_ref,
                  wg_ref, wbd_ref, wbc_ref, wbf_ref, wo_ref, o_ref, *, prompt_tiles, seq_tiles, ts):
    h = h_ref[...]
    tm, d = h.shape
    xn = _rms(h, gpre_ref[...]).astype(BF16)

    i = pl.program_id(0)
    is_sample = i >= prompt_tiles
    z = z_ref[...]
    row = lax.broadcasted_iota(I32, z.shape, 0)
    t = row & (jnp.where(is_sample, ts, tm) - 1)
    keep = jnp.where(jnp.logical_or(is_sample, i % seq_tiles == 0), 0.0, 1.0)
    prev = zprev_ref[...] * keep
    p1, p2 = prev[SUBLANES - 1:SUBLANES], prev[SUBLANES - 2:SUBLANES - 1]
    r1 = jnp.where(is_sample, e1_ref[...], jnp.broadcast_to(p1, z.shape))
    r2 = jnp.where(is_sample, e2_ref[...], jnp.where(t == 0, p2, p1))
    z1 = jnp.where(t == 0, r1, pltpu.roll(z, 1, axis=0))
    z2 = jnp.where(t < 2, r2, pltpu.roll(z, 2, axis=0))
    cw = cw_ref[...]
    conv = cw[0:1] * z2 + cw[1:2] * z1 + cw[2:3] * z

    branches = ((od_ref[...], wbd_ref), (zb_ref[...] * conv, wbc_ref), (of_ref[...], wbf_ref))
    merged = jnp.zeros_like(h)
    for n, (val, w_ref) in enumerate(branches):
        gate = jax.nn.sigmoid(_dot(xn, wg_ref[:, n * d:(n + 1) * d]))
        merged = merged + gate * _dot(val.astype(BF16), w_ref[...])
    o_ref[...] = h + _rms(_dot(merged.astype(BF16), wo_ref[...]), gpost_ref[...])


def _merge(h, od, of, zb, z, e1, e2, g_pre, g_post, conv_w, wg, wbd, wbc, wbf, wo, l, n_p, seq, ts, *, tm):
    n, d = h.shape
    assert seq % tm == 0 and tm % ts == 0 and tm & (tm - 1) == 0 and ts & (ts - 1) == 0 and CONV_K == 3
    p_tiles = n_p // tm
    tok = lambda w: pl.BlockSpec((tm, w), lambda i: (i, 0))
    lay = lambda a, b: pl.BlockSpec((None, a, b), lambda i: (l, 0, 0))
    halo = pl.BlockSpec((SUBLANES, CONV_WIDTH), lambda i: (jnp.maximum(i * (tm // SUBLANES) - 1, 0), 0))
    smp = pl.BlockSpec((tm, CONV_WIDTH), lambda i: (jnp.maximum(i - p_tiles, 0), 0))
    return pl.pallas_call(
        functools.partial(_merge_kernel, prompt_tiles=p_tiles, seq_tiles=seq // tm, ts=ts),
        grid=(n // tm,),
        in_specs=[tok(d), tok(Q_WIDTH), tok(Q_WIDTH), tok(CONV_WIDTH), tok(CONV_WIDTH), halo, smp, smp,
                  lay(1, d), lay(1, d), lay(CONV_K, CONV_WIDTH),
                  lay(d, N_BRANCHES * d), lay(Q_WIDTH, d), lay(CONV_WIDTH, d), lay(Q_WIDTH, d), lay(d, d)],
        out_specs=tok(d),
        out_shape=jax.ShapeDtypeStruct((n, d), F32),
        compiler_params=_params("parallel"),
        name="merge",
    )(h, od, of, zb, z, z, e1, e2, g_pre, g_post, conv_w, wg, wbd, wbc, wbf, wo)


def _rope_tables(pos):
    half = ROPE_DIM // 2
    inv = jnp.power(jnp.float32(ROPE_THETA), -jnp.arange(half, dtype=jnp.float32) * (2.0 / ROPE_DIM))
    ang = pos.astype(jnp.float32)[:, None] * inv[None, :]
    cos, sin = jnp.cos(ang), jnp.sin(ang)
    n = pos.shape[0]
    ones = jnp.ones((n, HEAD_DIM - ROPE_DIM), F32)
    zeros_r = jnp.zeros((n, HEAD_DIM - ROPE_DIM), F32)
    zeros_h = jnp.zeros((n, half), F32)
    cos_t = jnp.concatenate([cos, cos, ones], axis=1)
    sa_t = jnp.concatenate([-sin, zeros_h, zeros_r], axis=1)
    sb_t = jnp.concatenate([zeros_h, sin, zeros_r], axis=1)
    rep = LANES // HEAD_DIM
    return tuple(jnp.tile(x, (1, rep)) for x in (cos_t, sa_t, sb_t))


def _pick(n, prefs):
    for p in prefs:
        if n % p == 0:
            return p
    return n


def kernel(x_prompt, x_sample, cache_dsa_k, cache_dsa_v, cache_idx_k, cache_fox_k, cache_fox_v, cache_fox_logf,
           state_conv, page_table, g_ffn1_pre, g_ffn1_post, w_ffn1_gu, w_ffn1_dn, g_mix_pre, g_mix_post, w_in,
           b_forget, conv_w, w_br_dsa, w_br_conv, w_br_fox, w_out, g_ffn2_pre, g_ffn2_post, w_ffn2_gu, w_ffn2_dn):
    bsz, seq, d = x_prompt.shape
    r, ts, _ = x_sample.shape
    depth = w_in.shape[0]
    n_pool = cache_dsa_k.shape[1]
    n_pages = page_table.shape[1]
    past = n_pages * PAGE_SIZE
    n_p, n_s = bsz * seq, r * ts
    dff = w_ffn1_dn.shape[1]
    assert ts == SUBLANES and seq % LANES == 0

    tm = _pick(math.gcd(n_p, n_s), (512, 256, 128, 64, 32, 16, 8))
    tf = _pick(dff, (1408, 1024, 512, 256, 128))
    tq = _pick(seq, (128,))
    tk = _pick(seq, (512, 256, 128))
    nk = seq // tk

    sizes = (Q_WIDTH, KV_WIDTH, KV_WIDTH, IDX_HEADS * IDX_DIM, IDX_DIM, IDX_HEADS, CONV_WIDTH, CONV_WIDTH,
             CONV_WIDTH, Q_WIDTH, KV_WIDTH, KV_WIDTH, FOX_HEADS, N_BRANCHES * d)
    offs = [0]
    for s_ in sizes:
        offs.append(offs[-1] + s_)
    col = lambda i: w_in[:, :, offs[i]:offs[i + 1]]
    (c_qd, c_kd, c_vd, c_qi, c_ki, c_wi, c_zb, c_zc, c_zx, c_qf, c_kf, c_vf, c_fl, c_g) = [col(i) for i in range(14)]
    ki_pad = jnp.concatenate([c_ki, jnp.zeros((depth, d, LANES - IDX_DIM), F32)], axis=2)
    w_rope = jnp.concatenate([c_qd, c_kd, c_qi, ki_pad], axis=2).astype(BF16)
    w_plain = jnp.concatenate([c_vd, c_zb, c_qf, c_kf, c_vf], axis=2).astype(BF16)
    w_z = jnp.concatenate([c_zc, c_zx], axis=2).astype(BF16)
    w_wi, w_fl, w_g = c_wi.astype(BF16), c_fl.astype(BF16), c_g.astype(BF16)
    w1gu, w1dn = w_ffn1_gu.astype(BF16), w_ffn1_dn.astype(BF16)
    w2gu, w2dn = w_ffn2_gu.astype(BF16), w_ffn2_dn.astype(BF16)
    wbd, wbc, wbf, wo = (w.astype(BF16) for w in (w_br_dsa, w_br_conv, w_br_fox, w_out))
    row = lambda g: g.reshape(depth, 1, -1)
    g1pre, g1post, gmpre, gmpost, g2pre, g2post, bfg = (
        row(g) for g in (g_ffn1_pre, g_ffn1_post, g_mix_pre, g_mix_post, g_ffn2_pre, g_ffn2_post, b_forget))

    pos = jnp.concatenate([jnp.tile(jnp.arange(seq, dtype=I32), bsz),
                           jnp.tile(past + jnp.arange(ts, dtype=I32), r)])
    cos_t, sa_t, sb_t = _rope_tables(pos)

    page_t = lambda c: jnp.moveaxis(c, 2, -1).reshape(depth, n_pool, -1, PAGE_SIZE)
    caches_t = [page_t(c) for c in (cache_dsa_k, cache_dsa_v, cache_idx_k, cache_fox_k, cache_fox_v, cache_fox_logf)]

    x = jnp.concatenate([x_prompt.reshape(n_p, d), x_sample.reshape(n_s, d)], axis=0)
    stacks, tails_p, tails_s = None, [], []
    for l in range(depth):
        h = _ffn(x, g1pre, g1post, w1gu, w1dn, l, tm=tm, tf=tf)
        (qd, qi, zb, qf, z, wi), (vd_l, ki_l, vf_l), stacks = _inproj(
            h, gmpre, cos_t, sa_t, sb_t, bfg, w_rope, w_plain, w_z, w_wi, w_fl, l, bsz, seq, stacks, tm=tm)
        kd_t, vd_t, ki_t, kf_t, vf_t, lf_p, kd_s, vd_s, ki_s, kf_s, vf_s, lf_s = stacks

        o_dsa = _dsa_prompt(qd, qi.T, wi.T, kd_t, vd_l, ki_l, bsz, seq, l, tq=tq, tk=tk)
        lf_blocks = lf_p[l].reshape(bsz, seq // LANES, LANES, FOX_HEADS).transpose(0, 1, 3, 2)
        f_k = _cumsum_prompt(lf_blocks.reshape(bsz, seq // LANES * FOX_HEADS, LANES))
        f_q = jnp.swapaxes(f_k, 1, 2)
        f_kc = f_k.reshape(bsz, FOX_HEADS, nk, tk).transpose(0, 2, 1, 3)
        o_fox = _fox_prompt(qf, kf_t, vf_l, f_q, f_kc, bsz, seq, l, tq=tq, tk=tk)
        lf_new_t = jnp.pad(jnp.swapaxes(lf_s[l].reshape(r, ts, FOX_HEADS), 1, 2), ((0, 0), (0, 0), (0, LANES - ts)))
        o_dsa, o_fox = _sample_attn(page_table, caches_t, qd, qi, wi, kd_s, vd_s, ki_s, qf, kf_s, vf_s, lf_new_t,
                                    o_dsa, o_fox, n_p, l)

        st = state_conv[l].astype(F32)
        pad_t = lambda a: jnp.pad(a, ((0, 0), (0, ts - a.shape[1]), (0, 0))).reshape(n_s, CONV_WIDTH)
        h = _merge(h, o_dsa, o_fox, zb, z, pad_t(st[:, 1:]), pad_t(st), gmpre, gmpost, conv_w,
                   w_g, wbd, wbc, wbf, wo, l, n_p, seq, ts, tm=tm)
        x = _ffn(h, g2pre, g2post, w2gu, w2dn, l, tm=tm, tf=tf)

        tails_p.append(z[:n_p].reshape(bsz, seq, CONV_WIDTH)[:, seq - (CONV_K - 1):])
        tails_s.append(z[n_p:].reshape(r, ts, CONV_WIDTH)[:, ts - (CONV_K - 1):])

    heads_t = lambda a: a.reshape(depth, bsz, DSA_KV_HEADS, HEAD_DIM, seq).transpose(0, 1, 4, 2, 3)
    kd_t, vd_t, ki_t, kf_t, vf_t, lf_p, kd_s, vd_s, ki_s, kf_s, vf_s, lf_s = stacks
    sp = [heads_t(kd_t), heads_t(vd_t), ki_t.transpose(0, 1, 3, 2), heads_t(kf_t), heads_t(vf_t),
          lf_p.reshape(depth, bsz, seq, FOX_HEADS), jnp.stack(tails_p, axis=0)]
    heads_s = lambda a: a.reshape(depth, r, ts, DSA_KV_HEADS, HEAD_DIM)
    ss = [heads_s(kd_s), heads_s(vd_s), ki_s.reshape(depth, r, ts, IDX_DIM), heads_s(kf_s), heads_s(vf_s),
          lf_s.reshape(depth, r, ts, FOX_HEADS), jnp.stack(tails_s, axis=0)]
    return (x[:n_p].reshape(bsz, seq, d), x[n_p:].reshape(r, ts, d),
            sp[0], sp[1], sp[2], sp[3], sp[4], sp[5], sp[6],
            ss[0], ss[1], ss[2], ss[3], ss[4], ss[5], ss[6])
```

```python
import functools
import math

import jax
import jax.numpy as jnp
from jax import lax
from jax.experimental import pallas as pl
from jax.experimental.pallas import tpu as pltpu

HEAD_DIM = 64
DSA_HEADS = 8
DSA_KV_HEADS = 4
IDX_HEADS = 8
IDX_DIM = 64
TOPK_MAX = 256
CONV_WIDTH = 512
CONV_K = 3
FOX_HEADS = 8
FOX_KV_HEADS = 4
ROPE_THETA = 500000.0
ROPE_DIM = HEAD_DIM // 4
NORM_EPS = 1e-6
N_BRANCHES = 3
PAGE_SIZE = 128

LANES = 128
SUBLANES = 8
KV_WIDTH = DSA_KV_HEADS * HEAD_DIM
Q_WIDTH = DSA_HEADS * HEAD_DIM
VMEM_LIMIT = 56 * 1024 * 1024

F32 = jnp.float32
BF16 = jnp.bfloat16
I32 = jnp.int32
NEG = -1e30
INT_MIN = -(2 ** 31)
NEG_INF_KEY = INT_MIN + 0x7FFFFF
MERGE_ROW_CHAINS = 2
SEARCH_STAGES = (0, 24, 28, 32)

assert DSA_HEADS == FOX_HEADS and DSA_KV_HEADS == FOX_KV_HEADS
assert DSA_HEADS // DSA_KV_HEADS == 2 and 2 * HEAD_DIM == LANES


def _rms(x, g):
    return x * lax.rsqrt(jnp.mean(x * x, axis=-1, keepdims=True) + NORM_EPS) * g


def _nt_dot(a, b):
    return lax.dot_general(a, b, (((1,), (1,)), ((), ())), preferred_element_type=F32)


def _dot(a, b):
    return jnp.dot(a, b, preferred_element_type=F32)


def _params(*sem):
    return pltpu.CompilerParams(dimension_semantics=sem, vmem_limit_bytes=VMEM_LIMIT)


def _ffn_kernel(x_ref, gpre_ref, gpost_ref, w_gu_ref, w_dn_ref, o_ref, *, tf, chains):
    dff = w_dn_ref.shape[0]
    rows = x_ref.shape[0] // chains
    for c in range(chains):
        sl = slice(c * rows, (c + 1) * rows)
        x = x_ref[sl]
        xn = _rms(x, gpre_ref[...]).astype(BF16)
        acc = jnp.zeros_like(x)
        for j in range(dff // tf):
            a = _dot(xn, w_gu_ref[:, j * tf:(j + 1) * tf])
            b = _dot(xn, w_gu_ref[:, dff + j * tf:dff + (j + 1) * tf])
            act = (a * jax.nn.sigmoid(a) * b).astype(BF16)
            acc = acc + _dot(act, w_dn_ref[j * tf:(j + 1) * tf])
        o_ref[sl] = x + 0.5 * _rms(acc, gpost_ref[...])


def _ffn(x, g_pre, g_post, w_gu, w_dn, l, *, tm, tf):
    n, d = x.shape
    dff = w_dn.shape[1]
    return pl.pallas_call(
        functools.partial(_ffn_kernel, tf=tf, chains=2),
        grid=(n // tm,),
        in_specs=[
            pl.BlockSpec((tm, d), lambda i: (i, 0)),
            pl.BlockSpec((None, 1, d), lambda i: (l, 0, 0)),
            pl.BlockSpec((None, 1, d), lambda i: (l, 0, 0)),
            pl.BlockSpec((None, d, 2 * dff), lambda i: (l, 0, 0)),
            pl.BlockSpec((None, dff, d), lambda i: (l, 0, 0)),
        ],
        out_specs=pl.BlockSpec((tm, d), lambda i: (i, 0)),
        out_shape=jax.ShapeDtypeStruct((n, d), F32),
        compiler_params=_params("parallel"),
        name="ffn",
    )(x, g_pre, g_post, w_gu, w_dn)


ROPE_COLS = Q_WIDTH + KV_WIDTH + Q_WIDTH + LANES
PLAIN_OFFSETS = (0, KV_WIDTH, KV_WIDTH + CONV_WIDTH, KV_WIDTH + CONV_WIDTH + Q_WIDTH,
                 2 * KV_WIDTH + CONV_WIDTH + Q_WIDTH, 3 * KV_WIDTH + CONV_WIDTH + Q_WIDTH)
PLAIN_COLS = PLAIN_OFFSETS[-1]


def _log_sigmoid(x):
    return jnp.minimum(x, 0.0) - jnp.log1p(jnp.exp(-jnp.abs(x)))


def _inproj_kernel(x_ref, g_ref, cos_ref, sa_ref, sb_ref, bf_ref, wr_ref, wp_ref, wz_ref, wwi_ref, wfl_ref,
                   *refs, prompt_tiles):
    (qd_o, qi_o, zb_o, qf_o, z_o, wi_o, vd_l, ki_l, vf_l, kd_t, vd_t, ki_t, kf_t, vf_t, lf_p,
     kd_s, vd_s, ki_s, kf_s, vf_s, lf_s) = refs[-21:]
    xn = _rms(x_ref[...], g_ref[...]).astype(BF16)
    cos, sa, sb = cos_ref[...], sa_ref[...], sb_ref[...]

    r = _dot(xn, wr_ref[...])
    outs = []
    for c in range(ROPE_COLS // LANES):
        v = r[:, c * LANES:(c + 1) * LANES]
        up = pltpu.roll(v, LANES - ROPE_DIM // 2, axis=1)
        dn = pltpu.roll(v, ROPE_DIM // 2, axis=1)
        outs.append(v * cos + up * sa + dn * sb)
    nq = Q_WIDTH // LANES
    nk = KV_WIDTH // LANES
    qd_o[...] = jnp.concatenate(outs[:nq], axis=1)
    kd = jnp.concatenate(outs[nq:nq + nk], axis=1)
    qi_o[...] = jnp.concatenate(outs[nq + nk:2 * nq + nk], axis=1)
    ki_pad = outs[2 * nq + nk]
    ki = ki_pad[:, :IDX_DIM]

    p = _dot(xn, wp_ref[...])
    vd, zb, qf, kf, vf = (p[:, a:b] for a, b in zip(PLAIN_OFFSETS[:-1], PLAIN_OFFSETS[1:]))
    zb_o[...] = zb
    qf_o[...] = qf

    zz = _dot(xn, wz_ref[...])
    z_o[...] = zz[:, :CONV_WIDTH] * zz[:, CONV_WIDTH:]

    wi_o[...] = _dot(xn, wwi_ref[...]) * (IDX_HEADS ** -0.5)
    lf = _log_sigmoid(_dot(xn, wfl_ref[...]) + bf_ref[...])

    is_prompt = pl.program_id(0) < prompt_tiles
    @pl.when(is_prompt)
    def _():
        for ref, val in ((vd_l, vd), (ki_l, ki), (vf_l, vf), (lf_p, lf), (kd_t, kd.T), (vd_t, vd.T),
                         (ki_t, ki_pad.T[:IDX_DIM]), (kf_t, kf.T), (vf_t, vf.T)):
            ref[...] = val

    @pl.when(jnp.logical_not(is_prompt))
    def _():
        for ref, val in zip((kd_s, vd_s, ki_s, kf_s, vf_s, lf_s), (kd, vd, ki, kf, vf, lf)):
            ref[...] = val


def _inproj(x, g, cos, sa, sb, b_forget, wr, wp, wz, wwi, wfl, l, bsz, seq, stacks, *, tm):
    n, d = x.shape
    depth = wr.shape[0]
    n_p = bsz * seq
    p_tiles, s_tiles, tps = n_p // tm, (n - n_p) // tm, seq // tm
    pi = lambda i: jnp.minimum(i, p_tiles - 1)
    tok = lambda w: pl.BlockSpec((tm, w), lambda i: (i, 0))
    tok_l = lambda w: pl.BlockSpec((tm, w), lambda i: (pi(i), 0))
    tok_t = lambda w: pl.BlockSpec((None, None, w, tm), lambda i: (l, pi(i) // tps, 0, pi(i) % tps))
    tok_p = lambda w: pl.BlockSpec((None, tm, w), lambda i: (l, pi(i), 0))
    tok_s = lambda w: pl.BlockSpec((None, tm, w), lambda i: (l, jnp.maximum(i - p_tiles, 0), 0))
    lay = lambda a, b: pl.BlockSpec((None, a, b), lambda i: (l, 0, 0))
    all_w = (Q_WIDTH, Q_WIDTH, CONV_WIDTH, Q_WIDTH, CONV_WIDTH, IDX_HEADS)
    lay_w = (KV_WIDTH, IDX_DIM, KV_WIDTH)
    t_w = (KV_WIDTH, KV_WIDTH, IDX_DIM, KV_WIDTH, KV_WIDTH)
    row_w = (KV_WIDTH, KV_WIDTH, IDX_DIM, KV_WIDTH, KV_WIDTH, FOX_HEADS)
    in_specs = [tok(d), lay(1, d), tok(LANES), tok(LANES), tok(LANES), lay(1, FOX_HEADS),
                lay(d, ROPE_COLS), lay(d, PLAIN_COLS), lay(d, 2 * CONV_WIDTH), lay(d, IDX_HEADS), lay(d, FOX_HEADS)]
    args = [x, g, cos, sa, sb, b_forget, wr, wp, wz, wwi, wfl]
    n_plain = len(all_w) + len(lay_w)
    aliases = {}
    if stacks is not None:
        aliases = {len(args) + k: n_plain + k for k in range(len(stacks))}
        in_specs = in_specs + [pl.BlockSpec(memory_space=pl.ANY)] * len(stacks)
        args = args + list(stacks)
    sds = jax.ShapeDtypeStruct
    outs = pl.pallas_call(
        functools.partial(_inproj_kernel, prompt_tiles=p_tiles),
        grid=(n // tm,),
        in_specs=in_specs,
        out_specs=([tok(w) for w in all_w] + [tok_l(w) for w in lay_w] + [tok_t(w) for w in t_w]
                   + [tok_p(FOX_HEADS)] + [tok_s(w) for w in row_w]),
        out_shape=([sds((n, w), F32) for w in all_w] + [sds((n_p, w), F32) for w in lay_w]
                   + [sds((depth, bsz, w, seq), F32) for w in t_w] + [sds((depth, n_p, FOX_HEADS), F32)]
                   + [sds((depth, s_tiles * tm, w), F32) for w in row_w]),
        input_output_aliases=aliases,
        compiler_params=_params("arbitrary"),
        name="inproj",
    )(*args)
    return outs[:6], outs[6:n_plain], outs[n_plain:]


def _block_cumsum(x, n_blocks):
    rows = n_blocks * SUBLANES
    padded = -(-rows // LANES) * LANES
    if padded != rows:
        x = jnp.concatenate([x, jnp.zeros((padded - rows, LANES), F32)], axis=0)
    hi = functools.partial(jnp.dot, preferred_element_type=F32, precision=lax.Precision.HIGHEST)
    r0 = lax.broadcasted_iota(I32, (LANES, LANES), 0)
    c0 = lax.broadcasted_iota(I32, (LANES, LANES), 1)
    within = hi(x, jnp.where(r0 <= c0, 1.0, 0.0).astype(F32))
    totals = jnp.broadcast_to(within[:, LANES - 1:LANES], (padded, LANES))
    dist = lax.broadcasted_iota(I32, (padded, padded), 0) - lax.broadcasted_iota(I32, (padded, padded), 1)
    earlier = jnp.where((dist > 0) & ((dist & (SUBLANES - 1)) == 0), 1.0, 0.0).astype(F32)
    full = within + hi(earlier, totals)
    return [full[b * SUBLANES:(b + 1) * SUBLANES] for b in range(n_blocks)]


def _cumsum_kernel(x_ref, o_ref, *, n_blocks):
    o_ref[...] = jnp.concatenate(_block_cumsum(x_ref[...], n_blocks), axis=1)


def _cumsum_prompt(lf_blocks):
    b, rows, _ = lf_blocks.shape
    n_blocks = rows // FOX_HEADS
    t = n_blocks * LANES
    return pl.pallas_call(
        functools.partial(_cumsum_kernel, n_blocks=n_blocks),
        grid=(b,),
        in_specs=[pl.BlockSpec((None, rows, LANES), lambda i: (i, 0, 0))],
        out_specs=pl.BlockSpec((None, FOX_HEADS, t), lambda i: (i, 0, 0)),
        out_shape=jax.ShapeDtypeStruct((b, FOX_HEADS, t), F32),
        compiler_params=_params("parallel"),
        name="cumsum_prompt",
    )(lf_blocks)


def _q_block_diag(q):
    t = q.shape[0]
    lane = lax.broadcasted_iota(I32, (t, LANES), 1)
    zeros = jnp.zeros((t, LANES), F32)
    blocks = []
    for h in range(DSA_HEADS):
        g, r = divmod(h, 2)
        src = q[:, g * LANES:(g + 1) * LANES]
        if r != g % 2:
            src = pltpu.roll(src, HEAD_DIM, axis=1)
        keep = (lane < HEAD_DIM) if g % 2 == 0 else (lane >= HEAD_DIM)
        m = jnp.where(keep, src, 0.0)
        blocks.append(jnp.concatenate([m, zeros] if g // 2 == 0 else [zeros, m], axis=1))
    return jnp.concatenate(blocks, axis=0).astype(BF16)


def _head_rows(q):
    return jnp.concatenate([q[:, h * IDX_DIM:(h + 1) * IDX_DIM] for h in range(IDX_HEADS)], axis=0)


def _extract_heads(acc, t):
    outs = []
    for h in range(DSA_HEADS):
        g = h // 2
        outs.append(acc[h * t:(h + 1) * t, g * HEAD_DIM:(g + 1) * HEAD_DIM])
    return jnp.concatenate(outs, axis=1)


def _sort_key(score):
    bits = pltpu.bitcast(score, I32)
    return jnp.where(bits < 0, bits ^ 0x7FFFFFFF, bits)


def _indexer_score(s, wi, t):
    score = jnp.zeros((t, s.shape[1]), F32)
    for h in range(s.shape[0] // t):
        score = score + jnp.maximum(s[h * t:(h + 1) * t], 0.0) * wi[:, h:h + 1]
    return jnp.where(score == 0.0, 0.0, score)


def _fold_lanes(x):
    out = x[:, :LANES]
    for j in range(1, x.shape[1] // LANES):
        out = out + x[:, j * LANES:(j + 1) * LANES]
    return out


LOG2E = 1.4426950408889634


def _two_pass_attention(qbd, kt_ref, v_ref, loops, bias_fn, s_ref, mx_ref, l_ref, acc_ref, tq, tk):
    group = DSA_HEADS // DSA_KV_HEADS
    n_tiles = tk // LANES

    def fold(x, op):
        out = x[:, :, :LANES]
        for j in range(1, n_tiles):
            out = op(out, x[:, :, j * LANES:(j + 1) * LANES])
        return out

    def logit_chunk(c, carry, *, tag):
        kt = kt_ref[:, pl.ds(pl.multiple_of(c * tk, tk), tk)].astype(BF16)
        bias = bias_fn(c, tag)
        for g in range(DSA_HEADS // group):
            h0, h1 = g * group, (g + 1) * group
            s3 = _dot(qbd[h0 * tq:h1 * tq], kt).reshape(group, tq, tk) + bias(h0, h1)
            s_ref[c, h0:h1] = s3
            mx_ref[h0:h1] = jnp.maximum(mx_ref[h0:h1], fold(s3, jnp.maximum))
        return carry

    def pv_chunk(c, carry):
        v = v_ref[pl.ds(pl.multiple_of(c * tk, tk), tk), :].astype(BF16)
        for g in range(DSA_HEADS // group):
            h0, h1 = g * group, (g + 1) * group
            m = jnp.concatenate([mx_ref[h0:h1]] * n_tiles, axis=2)
            p = jnp.exp2(s_ref[c, h0:h1] - m)
            l_ref[h0:h1] += fold(p, jnp.add)
            acc_ref[h0:h1] += _dot(p.reshape(group * tq, tk).astype(BF16), v).reshape(group, tq, KV_WIDTH)
        return carry

    mx_ref[...] = jnp.full_like(mx_ref, NEG)
    for lo, hi, tag in loops:
        lax.fori_loop(lo, hi, functools.partial(logit_chunk, tag=tag), 0)
    mx_ref[...] = jnp.broadcast_to(jnp.max(mx_ref[...], axis=2, keepdims=True), mx_ref.shape)
    l_ref[...] = jnp.zeros_like(l_ref)
    acc_ref[...] = jnp.zeros_like(acc_ref)
    lax.fori_loop(loops[0][0], loops[-1][1], pv_chunk, 0)
    out = acc_ref[...] / jnp.sum(l_ref[...], axis=2, keepdims=True)
    return _extract_heads(out.reshape(DSA_HEADS * tq, KV_WIDTH), tq)


def _two_pass_scratch(tq, tk, nk):
    return [pltpu.VMEM((nk, DSA_HEADS, tq, tk), F32), pltpu.VMEM((DSA_HEADS, tq, LANES), F32),
            pltpu.VMEM((DSA_HEADS, tq, LANES), F32), pltpu.VMEM((DSA_HEADS, tq, KV_WIDTH), F32)]


def _fox_prompt_kernel(q_ref, kt_ref, v_ref, fq_ref, fk_ref, o_ref, s_ref, m_ref, l_ref, acc_ref, *, tq, tk):
    i = pl.program_id(1)
    qbd = _q_block_diag(q_ref[...] * (HEAD_DIM ** -0.5 * LOG2E))
    fq = fq_ref[...] * LOG2E
    fq_rep = jnp.stack([jnp.broadcast_to(fq[:, h:h + 1], (tq, LANES)) for h in range(FOX_HEADS)], axis=0)
    qpos = i * tq + lax.broadcasted_iota(I32, (tq, tk), 0)
    kiota = lax.broadcasted_iota(I32, (tq, tk), 1)

    def bias(c, masked):
        fk = fk_ref[c] * LOG2E
        fk3 = jnp.stack([fk[h:h + 1, :] for h in range(FOX_HEADS)], axis=0)
        causal = ((kiota + c * tk) <= qpos)[None]

        def heads(h0, h1):
            b = jnp.concatenate([fq_rep[h0:h1]] * (tk // LANES), axis=2) - fk3[h0:h1]
            return jnp.where(causal, b, NEG) if masked else b

        return heads

    n_full = (i * tq + 1) // tk
    loops = ((0, n_full, False), (n_full, (i * tq + tq - 1) // tk + 1, True))
    o_ref[...] = _two_pass_attention(qbd, kt_ref, v_ref, loops, bias, s_ref, m_ref, l_ref, acc_ref, tq, tk)


def _fox_prompt(q, kt, v, fq, fk, bsz, seq, l, *, tq, tk):
    n = q.shape[0]
    nq, nk = seq // tq, seq // tk
    return pl.pallas_call(
        functools.partial(_fox_prompt_kernel, tq=tq, tk=tk),
        grid=(bsz, nq),
        in_specs=[
            pl.BlockSpec((tq, Q_WIDTH), lambda bi, i: (bi * nq + i, 0)),
            pl.BlockSpec((None, None, KV_WIDTH, seq), lambda bi, i: (l, bi, 0, 0)),
            pl.BlockSpec((seq, KV_WIDTH), lambda bi, i: (bi, 0)),
            pl.BlockSpec((None, tq, FOX_HEADS), lambda bi, i: (bi, i, 0)),
            pl.BlockSpec((None, nk, FOX_HEADS, tk), lambda bi, i: (bi, 0, 0, 0)),
        ],
        out_specs=pl.BlockSpec((tq, Q_WIDTH), lambda bi, i: (bi * nq + i, 0)),
        out_shape=jax.ShapeDtypeStruct((n, Q_WIDTH), F32),
        scratch_shapes=_two_pass_scratch(tq, tk, nk),
        compiler_params=_params("parallel", "parallel"),
        name="fox_prompt",
    )(q, kt, v, fq, fk)


def _dsa_prompt_kernel(q_ref, qit_ref, wit_ref, kt_ref, v_ref, ki_ref, o_ref,
                       key_ref, thr_ref, cnt_ref, j_ref, s_ref, m_ref, l_ref, acc_ref, *, tq, tk, top_k, seq):
    i = pl.program_id(1)
    n_vis = (i * tq + tq - 1) // tk + 1
    kpos = lax.broadcasted_iota(I32, (tk, tq), 0)
    qpos = i * tq + lax.broadcasted_iota(I32, (tk, tq), 1)
    kf = float(top_k)

    qit = qit_ref[...].astype(BF16)
    qi_t = jnp.concatenate([qit[h * IDX_DIM:(h + 1) * IDX_DIM] for h in range(IDX_HEADS)], axis=1)
    wit = wit_ref[...] * (IDX_DIM ** -0.5)

    def score_chunk(c, carry):
        start = pl.multiple_of(c * tk, tk)
        s = _dot(ki_ref[pl.ds(start, tk), :].astype(BF16), qi_t)
        score = jnp.zeros((tk, tq), F32)
        for h in range(IDX_HEADS):
            score = score + jnp.maximum(s[:, h * tq:(h + 1) * tq], 0.0) * wit[h:h + 1, :]
        score = jnp.where(score == 0.0, 0.0, score)
        score = jnp.where((kpos + c * tk) <= qpos, score, -jnp.inf)
        key_ref[c] = _sort_key(score)
        return carry

    lax.fori_loop(0, n_vis, score_chunk, 0)

    def partial_count(mask):
        ones = jnp.where(mask, 1.0, 0.0).reshape(8, tk // (8 * SUBLANES), SUBLANES, tq)
        c = [jnp.sum(ones[g], axis=0) for g in range(8)]
        return ((c[0] + c[1]) + (c[2] + c[3])) + ((c[4] + c[5]) + (c[6] + c[7]))

    def count(*preds):
        def body(c, cnts):
            key, pos = key_ref[c], kpos + c * tk
            return tuple(cnt + partial_count(pred(key, pos)) for cnt, pred in zip(cnts, preds))
        zero = jnp.zeros((SUBLANES, tq), F32)
        totals = lax.fori_loop(0, n_vis, body, tuple(zero for _ in preds))
        return tuple(jnp.sum(x, axis=0, keepdims=True) for x in totals)

    def search(n_chunks):
        def step(b, carry):
            cand, cnt_at = carry
            trial = cand ^ lax.shift_left(jnp.int32(1), 31 - b)
            cnt = jnp.zeros((SUBLANES, tq), F32)
            for c in range(n_chunks):
                cnt = cnt + partial_count(key_ref[c] >= trial)
            total = jnp.sum(cnt, axis=0, keepdims=True)
            ok = total >= kf
            return jnp.where(ok, trial, cand), jnp.where(ok, total, cnt_at)

        carry = (jnp.full((1, tq), INT_MIN, I32), jnp.full((1, tq), float(n_chunks * tk), F32))
        for lo, hi in zip(SEARCH_STAGES[:-1], SEARCH_STAGES[1:]):
            if lo == 0:
                carry = lax.fori_loop(lo, hi, step, carry)
                thr_ref[...], cnt_ref[...] = carry
            else:
                @pl.when(jnp.max(jnp.where(cnt_ref[...] != kf, 1.0, 0.0)) > 0.0)
                def _():
                    thr_ref[...], cnt_ref[...] = lax.fori_loop(lo, hi, step, (thr_ref[...], cnt_ref[...]))

    for n_chunks in range(1, seq // tk + 1):
        @pl.when(n_vis == n_chunks)
        def _():
            search(n_chunks)

    thr = thr_ref[...]

    n_gt, n_ge = count(lambda key, pos: key > thr, lambda key, pos: key >= thr)
    need = kf - n_gt
    j_ref[...] = jnp.full_like(j_ref, seq)
    tied = jnp.where((n_ge > kf) & (thr != NEG_INF_KEY), 1.0, 0.0)

    @pl.when(jnp.max(tied) > 0.0)
    def _():
        def idx_step(b, lo):
            trial = lo | lax.shift_left(jnp.int32(1), int(math.log2(seq)) - 1 - b)
            (c,) = count(lambda key, pos: jnp.where(key == thr, pos, seq) < trial)
            return jnp.where(c < need, trial, lo)

        j_ref[...] = lax.fori_loop(0, int(math.log2(seq)), idx_step, jnp.zeros((1, tq), I32))

    jmax = j_ref[...]

    qbd = _q_block_diag(q_ref[...] * (HEAD_DIM ** -0.5 * LOG2E))

    def bias(c, tag):
        key = key_ref[c]
        tie_bias = jnp.where(jnp.where(key == thr, kpos + c * tk, seq + 1) <= jmax, 0.0, NEG)
        b = jnp.where(key > NEG_INF_KEY, jnp.where(key > thr, 0.0, tie_bias), NEG).T[None]
        return lambda h0, h1: b

    o_ref[...] = _two_pass_attention(qbd, kt_ref, v_ref, ((0, n_vis, None),), bias, s_ref, m_ref, l_ref, acc_ref,
                                     tq, tk)


def _dsa_prompt(q, qit, wit, kt, v, ki, bsz, seq, l, *, tq, tk):
    n = q.shape[0]
    nq, nk = seq // tq, seq // tk
    top_k = min(TOPK_MAX, seq // 4)
    return pl.pallas_call(
        functools.partial(_dsa_prompt_kernel, tq=tq, tk=tk, top_k=top_k, seq=seq),
        grid=(bsz, nq),
        in_specs=[
            pl.BlockSpec((tq, Q_WIDTH), lambda bi, i: (bi * nq + i, 0)),
            pl.BlockSpec((Q_WIDTH, tq), lambda bi, i: (0, bi * nq + i)),
            pl.BlockSpec((IDX_HEADS, tq), lambda bi, i: (0, bi * nq + i)),
            pl.BlockSpec((None, None, KV_WIDTH, seq), lambda bi, i: (l, bi, 0, 0)),
            pl.BlockSpec((seq, KV_WIDTH), lambda bi, i: (bi, 0)),
            pl.BlockSpec((seq, IDX_DIM), lambda bi, i: (bi, 0)),
        ],
        out_specs=pl.BlockSpec((tq, Q_WIDTH), lambda bi, i: (bi * nq + i, 0)),
        out_shape=jax.ShapeDtypeStruct((n, Q_WIDTH), F32),
        scratch_shapes=[pltpu.VMEM((nk, tk, tq), I32), pltpu.VMEM((1, tq), I32), pltpu.VMEM((1, tq), F32),
                        pltpu.VMEM((1, tq), I32)]
        + _two_pass_scratch(tq, tk, nk),
        compiler_params=_params("parallel", "parallel"),
        name="dsa_prompt",
    )(q, qit, wit, kt, v, ki)


def _page_specs(n_pages, rows, l):
    return [pl.BlockSpec((None, None, rows, PAGE_SIZE),
                         functools.partial(lambda b, pt, p: (l, pt[b * n_pages + p], 0, 0), p=p))
            for p in range(n_pages)]


def _pad_rows(x, rows):
    return jnp.concatenate([x, jnp.zeros((rows - x.shape[0], x.shape[1]), x.dtype)], axis=0)


def _paged_logits(lhs, pages_t, new_rows):
    blocks = [_dot(lhs, p[...].astype(BF16)) for p in pages_t]
    blocks.append(_nt_dot(lhs, _pad_rows(new_rows, PAGE_SIZE).astype(BF16)))
    return jnp.concatenate(blocks, axis=1)


def _softmax_pv(logits, v_pages_t, v_new, t):
    m = jnp.max(logits, axis=1, keepdims=True)
    p = jnp.exp(logits - m)
    l = jnp.sum(p, axis=1, keepdims=True)
    pb = p.astype(BF16)
    acc = jnp.zeros((logits.shape[0], KV_WIDTH), F32)
    for n, vp in enumerate(v_pages_t):
        acc = acc + _nt_dot(pb[:, n * PAGE_SIZE:(n + 1) * PAGE_SIZE], vp[...].astype(BF16))
    n = len(v_pages_t)
    acc = acc + _dot(pb[:, n * PAGE_SIZE:(n + 1) * PAGE_SIZE], _pad_rows(v_new, PAGE_SIZE).astype(BF16))
    return _extract_heads(acc / l, t)


def _visible(t, length, past):
    pos = lax.broadcasted_iota(I32, (t, length), 1)
    return pos, pos <= past + lax.broadcasted_iota(I32, (t, length), 0)


def _fox_sample_body(kp, vp, lp, q_ref, kn_ref, vn_ref, lfn_ref, t):
    n_pages = len(kp)
    past = n_pages * PAGE_SIZE
    length = past + PAGE_SIZE

    f_tiles = _block_cumsum(jnp.concatenate([p[...] for p in lp] + [lfn_ref[...]], axis=0), n_pages + 1)
    fk = jnp.concatenate(f_tiles, axis=1)
    lane = lax.broadcasted_iota(I32, (t, LANES), 1)
    diag = lane == lax.broadcasted_iota(I32, (t, LANES), 0)

    qbd = _q_block_diag(q_ref[...] * (HEAD_DIM ** -0.5))
    s = _paged_logits(qbd, kp, kn_ref[...])
    _, visible = _visible(t, length, past)
    rows = []
    for h in range(FOX_HEADS):
        f_new = jnp.broadcast_to(f_tiles[n_pages][h:h + 1, :], (t, LANES))
        fq = jnp.sum(jnp.where(diag, f_new, 0.0), axis=1, keepdims=True)
        sh = s[h * t:(h + 1) * t] + (fq - fk[h:h + 1, :])
        rows.append(jnp.where(visible, sh, NEG))
    return _softmax_pv(jnp.concatenate(rows, axis=0), vp, vn_ref[...], t)


def _sample_topk(key, pos, top_k, length, j_ref):
    t = key.shape[0]
    kf = float(top_k)
    count = lambda mask: jnp.sum(jnp.where(mask, 1.0, 0.0), axis=1, keepdims=True)

    def digit_pass(cand, shift, n_trials):
        digit = jnp.zeros((t, 1), I32)
        for j in range(1, n_trials + 1):
            trial = cand + jnp.left_shift(jnp.int32(j), shift)
            digit = digit + jnp.where(count(key >= trial) >= kf, 1, 0)
        return cand + jnp.left_shift(digit, shift)

    thr = digit_pass(jnp.full((t, 1), INT_MIN, I32), 30, 3)
    for shift in range(27, -1, -3):
        thr = digit_pass(thr, shift, 7)

    need = kf - count(key > thr)
    n_ge = count(key >= thr)
    j_ref[...] = jnp.full_like(j_ref, length)
    tied = jnp.where((n_ge > kf) & (thr != NEG_INF_KEY), 1.0, 0.0)

    @pl.when(jnp.max(tied) > 0.0)
    def _():
        n_bits = int(math.ceil(math.log2(length)))
        tie_pos = jnp.where(key == thr, pos, length)

        def idx_step(b, lo):
            trial = lo | lax.shift_left(jnp.int32(1), n_bits - 1 - b)
            return jnp.where(count(tie_pos < trial) < need, trial, lo)

        j_ref[...] = lax.fori_loop(0, n_bits, idx_step, jnp.zeros((t, 1), I32))

    return thr, j_ref[...]


def _dsa_sample_body(kp, vp, ip, q_ref, qi_ref, wi_ref, kn_ref, vn_ref, in_ref, j_ref, t, top_k):
    n_pages = len(kp)
    past = n_pages * PAGE_SIZE
    length = past + PAGE_SIZE

    qi = _head_rows(qi_ref[...]).astype(BF16)
    score = _indexer_score(_paged_logits(qi, ip, in_ref[...]), wi_ref[...] * (IDX_DIM ** -0.5), t)
    pos, visible = _visible(t, length, past)
    key = _sort_key(jnp.where(visible, score, -jnp.inf))
    thr, jmax = _sample_topk(key, pos, top_k, length, j_ref)
    tie_bias = jnp.where(jnp.where(key == thr, pos, length + 1) <= jmax, 0.0, NEG)
    bias = jnp.where(key > NEG_INF_KEY, jnp.where(key > thr, 0.0, tie_bias), NEG)

    qbd = _q_block_diag(q_ref[...] * (HEAD_DIM ** -0.5))
    logits = _paged_logits(qbd, kp, kn_ref[...]) + jnp.concatenate([bias] * DSA_HEADS, axis=0)
    return _softmax_pv(logits, vp, vn_ref[...], t)


def _sample_attn_kernel(pt_ref, *refs, n_pages, t, top_k):
    pages = [refs[n * n_pages:(n + 1) * n_pages] for n in range(6)]
    kd, vd, ki, kf, vf, lf = pages
    (qd_ref, qi_ref, wi_ref, kdn_ref, vdn_ref, kin_ref, qf_ref, kfn_ref, vfn_ref, lfn_ref, _, _,
     od_ref, of_ref, j_ref) = refs[6 * n_pages:]
    of_ref[...] = _fox_sample_body(kf, vf, lf, qf_ref, kfn_ref, vfn_ref, lfn_ref, t)
    od_ref[...] = _dsa_sample_body(kd, vd, ki, qd_ref, qi_ref, wi_ref, kdn_ref, vdn_ref, kin_ref, j_ref, t, top_k)


def _sample_attn(page_table, caches_t, qd, qi, wi, kd, vd, ki, qf, kf, vf, lf_new_t, o_dsa, o_fox, n_p, l):
    r, n_pages = page_table.shape
    t = (qd.shape[0] - n_p) // r
    off = n_p // t
    top_k = min(TOPK_MAX, (n_pages * PAGE_SIZE + t) // 4)
    req = lambda w: pl.BlockSpec((t, w), lambda bi, pt: (off + bi, 0))
    page_in = []
    for c in caches_t:
        page_in += _page_specs(n_pages, c.shape[2], l)
    new = lambda w: pl.BlockSpec((None, t, w), lambda bi, pt: (l, bi, 0))
    row_specs = [req(Q_WIDTH), req(Q_WIDTH), req(IDX_HEADS), new(KV_WIDTH), new(KV_WIDTH), new(IDX_DIM),
                 req(Q_WIDTH), new(KV_WIDTH), new(KV_WIDTH)]
    grid_spec = pltpu.PrefetchScalarGridSpec(
        num_scalar_prefetch=1,
        grid=(r,),
        in_specs=(page_in + row_specs
                  + [pl.BlockSpec((None, FOX_HEADS, LANES), lambda bi, pt: (bi, 0, 0)),
                     pl.BlockSpec(memory_space=pl.ANY), pl.BlockSpec(memory_space=pl.ANY)]),
        out_specs=[req(Q_WIDTH), req(Q_WIDTH)],
        scratch_shapes=[pltpu.VMEM((t, 1), I32)],
    )
    n_in = 1 + 6 * n_pages + len(row_specs) + 3
    page_args = [c for c in caches_t for _ in range(n_pages)]
    return pl.pallas_call(
        functools.partial(_sample_attn_kernel, n_pages=n_pages, t=t, top_k=top_k),
        grid_spec=grid_spec,
        out_shape=[jax.ShapeDtypeStruct(o_dsa.shape, F32), jax.ShapeDtypeStruct(o_fox.shape, F32)],
        input_output_aliases={n_in - 2: 0, n_in - 1: 1},
        compiler_params=_params("parallel"),
        name="sample_attn",
    )(page_table.reshape(-1), *page_args, qd, qi, wi, kd, vd, ki, qf, kf, vf, lf_new_t, o_dsa, o_fox)


def _merge_kernel(h_ref, od_ref, of_ref, zb_ref, z_ref, zprev_ref, e1_ref, e2_ref, gpre_ref, gpost_ref, cw_ref,
                  wg_ref, wbd_ref, wbc_ref, wbf_ref, wo_ref, o_ref, *, prompt_tiles, seq_tiles, ts):
    tm, d = h_ref.shape

    i = pl.program_id(0)
    is_sample = i >= prompt_tiles
    z = z_ref[...]
    row = lax.broadcasted_iota(I32, z.shape, 0)
    t = row & (jnp.where(is_sample, ts, tm) - 1)
    keep = jnp.where(jnp.logical_or(is_sample, i % seq_tiles == 0), 0.0, 1.0)
    prev = zprev_ref[...] * keep
    p1, p2 = prev[SUBLANES - 1:SUBLANES], prev[SUBLANES - 2:SUBLANES - 1]
    r1 = jnp.where(is_sample, e1_ref[...], jnp.broadcast_to(p1, z.shape))
    r2 = jnp.where(is_sample, e2_ref[...], jnp.where(t == 0, p2, p1))
    z1 = jnp.where(t == 0, r1, pltpu.roll(z, 1, axis=0))
    z2 = jnp.where(t < 2, r2, pltpu.roll(z, 2, axis=0))
    cw = cw_ref[...]
    conv = cw[0:1] * z2 + cw[1:2] * z1 + cw[2:3] * z

    o_conv = zb_ref[...] * conv

    rows = tm // MERGE_ROW_CHAINS
    for c in range(MERGE_ROW_CHAINS):
        sl = slice(c * rows, (c + 1) * rows)
        h = h_ref[sl]
        xn = _rms(h, gpre_ref[...]).astype(BF16)
        branches = ((od_ref[sl], wbd_ref), (o_conv[sl], wbc_ref), (of_ref[sl], wbf_ref))
        merged = jnp.zeros_like(h)
        for n, (val, w_ref) in enumerate(branches):
            gate = jax.nn.sigmoid(_dot(xn, wg_ref[:, n * d:(n + 1) * d]))
            merged = merged + gate * _dot(val.astype(BF16), w_ref[...])
        o_ref[sl] = h + _rms(_dot(merged.astype(BF16), wo_ref[...]), gpost_ref[...])


def _merge(h, od, of, zb, z, e1, e2, g_pre, g_post, conv_w, wg, wbd, wbc, wbf, wo, l, n_p, seq, ts, *, tm):
    n, d = h.shape
    assert seq % tm == 0 and tm % ts == 0 and tm & (tm - 1) == 0 and ts & (ts - 1) == 0 and CONV_K == 3
    p_tiles = n_p // tm
    tok = lambda w: pl.BlockSpec((tm, w), lambda i: (i, 0))
    lay = lambda a, b: pl.BlockSpec((None, a, b), lambda i: (l, 0, 0))
    halo = pl.BlockSpec((SUBLANES, CONV_WIDTH), lambda i: (jnp.maximum(i * (tm // SUBLANES) - 1, 0), 0))
    smp = pl.BlockSpec((tm, CONV_WIDTH), lambda i: (jnp.maximum(i - p_tiles, 0), 0))
    return pl.pallas_call(
        functools.partial(_merge_kernel, prompt_tiles=p_tiles, seq_tiles=seq // tm, ts=ts),
        grid=(n // tm,),
        in_specs=[tok(d), tok(Q_WIDTH), tok(Q_WIDTH), tok(CONV_WIDTH), tok(CONV_WIDTH), halo, smp, smp,
                  lay(1, d), lay(1, d), lay(CONV_K, CONV_WIDTH),
                  lay(d, N_BRANCHES * d), lay(Q_WIDTH, d), lay(CONV_WIDTH, d), lay(Q_WIDTH, d), lay(d, d)],
        out_specs=tok(d),
        out_shape=jax.ShapeDtypeStruct((n, d), F32),
        compiler_params=_params("parallel"),
        name="merge",
    )(h, od, of, zb, z, z, e1, e2, g_pre, g_post, conv_w, wg, wbd, wbc, wbf, wo)


def _rope_tables(pos):
    half = ROPE_DIM // 2
    inv = jnp.power(jnp.float32(ROPE_THETA), -jnp.arange(half, dtype=jnp.float32) * (2.0 / ROPE_DIM))
    ang = pos.astype(jnp.float32)[:, None] * inv[None, :]
    cos, sin = jnp.cos(ang), jnp.sin(ang)
    n = pos.shape[0]
    ones = jnp.ones((n, HEAD_DIM - ROPE_DIM), F32)
    zeros_r = jnp.zeros((n, HEAD_DIM - ROPE_DIM), F32)
    zeros_h = jnp.zeros((n, half), F32)
    cos_t = jnp.concatenate([cos, cos, ones], axis=1)
    sa_t = jnp.concatenate([-sin, zeros_h, zeros_r], axis=1)
    sb_t = jnp.concatenate([zeros_h, sin, zeros_r], axis=1)
    rep = LANES // HEAD_DIM
    return tuple(jnp.tile(x, (1, rep)) for x in (cos_t, sa_t, sb_t))


def _pick(n, prefs):
    for p in prefs:
        if n % p == 0:
            return p
    return n


def kernel(x_prompt, x_sample, cache_dsa_k, cache_dsa_v, cache_idx_k, cache_fox_k, cache_fox_v, cache_fox_logf,
           state_conv, page_table, g_ffn1_pre, g_ffn1_post, w_ffn1_gu, w_ffn1_dn, g_mix_pre, g_mix_post, w_in,
           b_forget, conv_w, w_br_dsa, w_br_conv, w_br_fox, w_out, g_ffn2_pre, g_ffn2_post, w_ffn2_gu, w_ffn2_dn):
    bsz, seq, d = x_prompt.shape
    r, ts, _ = x_sample.shape
    depth = w_in.shape[0]
    n_pool = cache_dsa_k.shape[1]
    n_pages = page_table.shape[1]
    past = n_pages * PAGE_SIZE
    n_p, n_s = bsz * seq, r * ts
    dff = w_ffn1_dn.shape[1]
    assert ts == SUBLANES and seq % LANES == 0

    tm = _pick(math.gcd(n_p, n_s), (512, 256, 128, 64, 32, 16, 8))
    tf = _pick(dff, (1408, 1024, 512, 256, 128))
    tq = _pick(seq, (128,))
    tk = _pick(seq, (512, 256, 128))
    nk = seq // tk

    sizes = (Q_WIDTH, KV_WIDTH, KV_WIDTH, IDX_HEADS * IDX_DIM, IDX_DIM, IDX_HEADS, CONV_WIDTH, CONV_WIDTH,
             CONV_WIDTH, Q_WIDTH, KV_WIDTH, KV_WIDTH, FOX_HEADS, N_BRANCHES * d)
    offs = [0]
    for s_ in sizes:
        offs.append(offs[-1] + s_)
    col = lambda i: w_in[:, :, offs[i]:offs[i + 1]]
    (c_qd, c_kd, c_vd, c_qi, c_ki, c_wi, c_zb, c_zc, c_zx, c_qf, c_kf, c_vf, c_fl, c_g) = [col(i) for i in range(14)]
    ki_pad = jnp.concatenate([c_ki, jnp.zeros((depth, d, LANES - IDX_DIM), F32)], axis=2)
    w_rope = jnp.concatenate([c_qd, c_kd, c_qi, ki_pad], axis=2).astype(BF16)
    w_plain = jnp.concatenate([c_vd, c_zb, c_qf, c_kf, c_vf], axis=2).astype(BF16)
    w_z = jnp.concatenate([c_zc, c_zx], axis=2).astype(BF16)
    w_wi, w_fl, w_g = c_wi.astype(BF16), c_fl.astype(BF16), c_g.astype(BF16)
    w1gu, w1dn = w_ffn1_gu.astype(BF16), w_ffn1_dn.astype(BF16)
    w2gu, w2dn = w_ffn2_gu.astype(BF16), w_ffn2_dn.astype(BF16)
    wbd, wbc, wbf, wo = (w.astype(BF16) for w in (w_br_dsa, w_br_conv, w_br_fox, w_out))
    row = lambda g: g.reshape(depth, 1, -1)
    g1pre, g1post, gmpre, gmpost, g2pre, g2post, bfg = (
        row(g) for g in (g_ffn1_pre, g_ffn1_post, g_mix_pre, g_mix_post, g_ffn2_pre, g_ffn2_post, b_forget))

    pos = jnp.concatenate([jnp.tile(jnp.arange(seq, dtype=I32), bsz),
                           jnp.tile(past + jnp.arange(ts, dtype=I32), r)])
    cos_t, sa_t, sb_t = _rope_tables(pos)

    page_t = lambda c: jnp.moveaxis(c, 2, -1).reshape(depth, n_pool, -1, PAGE_SIZE)
    caches_t = [page_t(c) for c in (cache_dsa_k, cache_dsa_v, cache_idx_k, cache_fox_k, cache_fox_v, cache_fox_logf)]

    x = jnp.concatenate([x_prompt.reshape(n_p, d), x_sample.reshape(n_s, d)], axis=0)
    stacks, tails_p, tails_s = None, [], []
    for l in range(depth):
        h = _ffn(x, g1pre, g1post, w1gu, w1dn, l, tm=tm, tf=tf)
        (qd, qi, zb, qf, z, wi), (vd_l, ki_l, vf_l), stacks = _inproj(
            h, gmpre, cos_t, sa_t, sb_t, bfg, w_rope, w_plain, w_z, w_wi, w_fl, l, bsz, seq, stacks, tm=tm)
        kd_t, vd_t, ki_t, kf_t, vf_t, lf_p, kd_s, vd_s, ki_s, kf_s, vf_s, lf_s = stacks

        o_dsa = _dsa_prompt(qd, qi.T, wi.T, kd_t, vd_l, ki_l, bsz, seq, l, tq=tq, tk=tk)
        lf_blocks = lf_p[l].reshape(bsz, seq // LANES, LANES, FOX_HEADS).transpose(0, 1, 3, 2)
        f_k = _cumsum_prompt(lf_blocks.reshape(bsz, seq // LANES * FOX_HEADS, LANES))
        f_q = jnp.swapaxes(f_k, 1, 2)
        f_kc = f_k.reshape(bsz, FOX_HEADS, nk, tk).transpose(0, 2, 1, 3)
        o_fox = _fox_prompt(qf, kf_t, vf_l, f_q, f_kc, bsz, seq, l, tq=tq, tk=tk)
        lf_new_t = jnp.pad(jnp.swapaxes(lf_s[l].reshape(r, ts, FOX_HEADS), 1, 2), ((0, 0), (0, 0), (0, LANES - ts)))
        o_dsa, o_fox = _sample_attn(page_table, caches_t, qd, qi, wi, kd_s, vd_s, ki_s, qf, kf_s, vf_s, lf_new_t,
                                    o_dsa, o_fox, n_p, l)

        st = state_conv[l].astype(F32)
        pad_t = lambda a: jnp.pad(a, ((0, 0), (0, ts - a.shape[1]), (0, 0))).reshape(n_s, CONV_WIDTH)
        h = _merge(h, o_dsa, o_fox, zb, z, pad_t(st[:, 1:]), pad_t(st), gmpre, gmpost, conv_w,
                   w_g, wbd, wbc, wbf, wo, l, n_p, seq, ts, tm=tm)
        x = _ffn(h, g2pre, g2post, w2gu, w2dn, l, tm=tm, tf=tf)

        tails_p.append(z[:n_p].reshape(bsz, seq, CONV_WIDTH)[:, seq - (CONV_K - 1):])
        tails_s.append(z[n_p:].reshape(r, ts, CONV_WIDTH)[:, ts - (CONV_K - 1):])

    heads_t = lambda a: a.reshape(depth, bsz, DSA_KV_HEADS, HEAD_DIM, seq).transpose(0, 1, 4, 2, 3)
    kd_t, vd_t, ki_t, kf_t, vf_t, lf_p, kd_s, vd_s, ki_s, kf_s, vf_s, lf_s = stacks
    sp = [heads_t(kd_t), heads_t(vd_t), ki_t.transpose(0, 1, 3, 2), heads_t(kf_t), heads_t(vf_t),
          lf_p.reshape(depth, bsz, seq, FOX_HEADS), jnp.stack(tails_p, axis=0)]
    heads_s = lambda a: a.reshape(depth, r, ts, DSA_KV_HEADS, HEAD_DIM)
    ss = [heads_s(kd_s), heads_s(vd_s), ki_s.reshape(depth, r, ts, IDX_DIM), heads_s(kf_s), heads_s(vf_s),
          lf_s.reshape(depth, r, ts, FOX_HEADS), jnp.stack(tails_s, axis=0)]
    return (x[:n_p].reshape(bsz, seq, d), x[n_p:].reshape(r, ts, d),
            sp[0], sp[1], sp[2], sp[3], sp[4], sp[5], sp[6],
            ss[0], ss[1], ss[2], ss[3], ss[4], ss[5], ss[6])
```

```python
import functools
import math

import jax
import jax.numpy as jnp
from jax import lax
from jax.experimental import pallas as pl
from jax.experimental.pallas import tpu as pltpu

HEAD_DIM = 64
DSA_HEADS = 8
DSA_KV_HEADS = 4
IDX_HEADS = 8
IDX_DIM = 64
TOPK_MAX = 256
CONV_WIDTH = 512
CONV_K = 3
FOX_HEADS = 8
FOX_KV_HEADS = 4
ROPE_THETA = 500000.0
ROPE_DIM = HEAD_DIM // 4
NORM_EPS = 1e-6
N_BRANCHES = 3
PAGE_SIZE = 128

LANES = 128
SUBLANES = 8
KV_WIDTH = DSA_KV_HEADS * HEAD_DIM
Q_WIDTH = DSA_HEADS * HEAD_DIM
VMEM_LIMIT = 56 * 1024 * 1024

F32 = jnp.float32
BF16 = jnp.bfloat16
I32 = jnp.int32
NEG = -1e30
INT_MIN = -(2 ** 31)
NEG_INF_KEY = INT_MIN + 0x7FFFFF
MERGE_ROW_CHAINS = 2
SEARCH_STAGES = (0, 24, 28, 32)

assert DSA_HEADS == FOX_HEADS and DSA_KV_HEADS == FOX_KV_HEADS
assert DSA_HEADS // DSA_KV_HEADS == 2 and 2 * HEAD_DIM == LANES


def _rms(x, g):
    return x * lax.rsqrt(jnp.mean(x * x, axis=-1, keepdims=True) + NORM_EPS) * g


def _nt_dot(a, b):
    return lax.dot_general(a, b, (((1,), (1,)), ((), ())), preferred_element_type=F32)


def _dot(a, b):
    return jnp.dot(a, b, preferred_element_type=F32)


def _params(*sem):
    return pltpu.CompilerParams(dimension_semantics=sem, vmem_limit_bytes=VMEM_LIMIT)


def _ffn_kernel(x_ref, gpre_ref, gpost_ref, w_gu_ref, w_dn_ref, o_ref, *, tf, chains):
    dff = w_dn_ref.shape[0]
    rows = x_ref.shape[0] // chains
    for c in range(chains):
        sl = slice(c * rows, (c + 1) * rows)
        x = x_ref[sl]
        xn = _rms(x, gpre_ref[...]).astype(BF16)
        acc = jnp.zeros_like(x)
        for j in range(dff // tf):
            a = _dot(xn, w_gu_ref[:, j * tf:(j + 1) * tf])
            b = _dot(xn, w_gu_ref[:, dff + j * tf:dff + (j + 1) * tf])
            act = (a * jax.nn.sigmoid(a) * b).astype(BF16)
            acc = acc + _dot(act, w_dn_ref[j * tf:(j + 1) * tf])
        o_ref[sl] = x + 0.5 * _rms(acc, gpost_ref[...])


def _ffn(x, g_pre, g_post, w_gu, w_dn, l, *, tm, tf):
    n, d = x.shape
    dff = w_dn.shape[1]
    return pl.pallas_call(
        functools.partial(_ffn_kernel, tf=tf, chains=2),
        grid=(n // tm,),
        in_specs=[
            pl.BlockSpec((tm, d), lambda i: (i, 0)),
            pl.BlockSpec((None, 1, d), lambda i: (l, 0, 0)),
            pl.BlockSpec((None, 1, d), lambda i: (l, 0, 0)),
            pl.BlockSpec((None, d, 2 * dff), lambda i: (l, 0, 0)),
            pl.BlockSpec((None, dff, d), lambda i: (l, 0, 0)),
        ],
        out_specs=pl.BlockSpec((tm, d), lambda i: (i, 0)),
        out_shape=jax.ShapeDtypeStruct((n, d), F32),
        compiler_params=_params("parallel"),
        name="ffn",
    )(x, g_pre, g_post, w_gu, w_dn)


ROPE_COLS = Q_WIDTH + KV_WIDTH + Q_WIDTH + LANES
PLAIN_OFFSETS = (0, KV_WIDTH, KV_WIDTH + CONV_WIDTH, KV_WIDTH + CONV_WIDTH + Q_WIDTH,
                 2 * KV_WIDTH + CONV_WIDTH + Q_WIDTH, 3 * KV_WIDTH + CONV_WIDTH + Q_WIDTH)
PLAIN_COLS = PLAIN_OFFSETS[-1]


def _log_sigmoid(x):
    return jnp.minimum(x, 0.0) - jnp.log1p(jnp.exp(-jnp.abs(x)))


def _inproj_kernel(x_ref, g_ref, cos_ref, sa_ref, sb_ref, bf_ref, wr_ref, wp_ref, wz_ref, wwi_ref, wfl_ref,
                   *refs, prompt_tiles):
    (qd_o, qi_o, zb_o, qf_o, z_o, wi_o, vd_l, ki_l, vf_l, kd_t, vd_t, ki_t, kf_t, vf_t, lf_p,
     kd_s, vd_s, ki_s, kf_s, vf_s, lf_s) = refs[-21:]
    xn = _rms(x_ref[...], g_ref[...]).astype(BF16)
    cos, sa, sb = cos_ref[...], sa_ref[...], sb_ref[...]

    r = _dot(xn, wr_ref[...])
    outs = []
    for c in range(ROPE_COLS // LANES):
        v = r[:, c * LANES:(c + 1) * LANES]
        up = pltpu.roll(v, LANES - ROPE_DIM // 2, axis=1)
        dn = pltpu.roll(v, ROPE_DIM // 2, axis=1)
        outs.append(v * cos + up * sa + dn * sb)
    nq = Q_WIDTH // LANES
    nk = KV_WIDTH // LANES
    qd_o[...] = jnp.concatenate(outs[:nq], axis=1)
    kd = jnp.concatenate(outs[nq:nq + nk], axis=1)
    qi_o[...] = jnp.concatenate(outs[nq + nk:2 * nq + nk], axis=1)
    ki_pad = outs[2 * nq + nk]
    ki = ki_pad[:, :IDX_DIM]

    p = _dot(xn, wp_ref[...])
    vd, zb, qf, kf, vf = (p[:, a:b] for a, b in zip(PLAIN_OFFSETS[:-1], PLAIN_OFFSETS[1:]))
    zb_o[...] = zb
    qf_o[...] = qf

    zz = _dot(xn, wz_ref[...])
    z_o[...] = zz[:, :CONV_WIDTH] * zz[:, CONV_WIDTH:]

    wi_o[...] = _dot(xn, wwi_ref[...]) * (IDX_HEADS ** -0.5)
    lf = _log_sigmoid(_dot(xn, wfl_ref[...]) + bf_ref[...])

    transposed = (kd.T, vd.T, ki_pad.T[:IDX_DIM], kf.T, vf.T)

    is_prompt = pl.program_id(0) < prompt_tiles
    @pl.when(is_prompt)
    def _():
        for ref, val in ((vd_l, vd), (ki_l, ki), (vf_l, vf), (lf_p, lf)):
            ref[...] = val
        for ref, val in zip((kd_t, vd_t, ki_t, kf_t, vf_t), transposed):
            ref[...] = val

    @pl.when(jnp.logical_not(is_prompt))
    def _():
        for ref, val in zip((kd_s, vd_s, ki_s, kf_s, vf_s, lf_s), (kd, vd, ki, kf, vf, lf)):
            ref[...] = val


def _inproj(x, g, cos, sa, sb, b_forget, wr, wp, wz, wwi, wfl, l, bsz, seq, stacks, *, tm):
    n, d = x.shape
    depth = wr.shape[0]
    n_p = bsz * seq
    p_tiles, s_tiles, tps = n_p // tm, (n - n_p) // tm, seq // tm
    pi = lambda i: jnp.minimum(i, p_tiles - 1)
    tok = lambda w: pl.BlockSpec((tm, w), lambda i: (i, 0))
    tok_l = lambda w: pl.BlockSpec((tm, w), lambda i: (pi(i), 0))
    tok_t = lambda w: pl.BlockSpec((None, None, w, tm), lambda i: (l, pi(i) // tps, 0, pi(i) % tps))
    tok_p = lambda w: pl.BlockSpec((None, tm, w), lambda i: (l, pi(i), 0))
    tok_s = lambda w: pl.BlockSpec((None, tm, w), lambda i: (l, jnp.maximum(i - p_tiles, 0), 0))
    lay = lambda a, b: pl.BlockSpec((None, a, b), lambda i: (l, 0, 0))
    all_w = (Q_WIDTH, Q_WIDTH, CONV_WIDTH, Q_WIDTH, CONV_WIDTH, IDX_HEADS)
    lay_w = (KV_WIDTH, IDX_DIM, KV_WIDTH)
    t_w = (KV_WIDTH, KV_WIDTH, IDX_DIM, KV_WIDTH, KV_WIDTH)
    row_w = (KV_WIDTH, KV_WIDTH, IDX_DIM, KV_WIDTH, KV_WIDTH, FOX_HEADS)
    in_specs = [tok(d), lay(1, d), tok(LANES), tok(LANES), tok(LANES), lay(1, FOX_HEADS),
                lay(d, ROPE_COLS), lay(d, PLAIN_COLS), lay(d, 2 * CONV_WIDTH), lay(d, IDX_HEADS), lay(d, FOX_HEADS)]
    args = [x, g, cos, sa, sb, b_forget, wr, wp, wz, wwi, wfl]
    n_plain = len(all_w) + len(lay_w)
    aliases = {}
    if stacks is not None:
        aliases = {len(args) + k: n_plain + k for k in range(len(stacks))}
        in_specs = in_specs + [pl.BlockSpec(memory_space=pl.ANY)] * len(stacks)
        args = args + list(stacks)
    sds = jax.ShapeDtypeStruct
    outs = pl.pallas_call(
        functools.partial(_inproj_kernel, prompt_tiles=p_tiles),
        grid=(n // tm,),
        in_specs=in_specs,
        out_specs=([tok(w) for w in all_w] + [tok_l(w) for w in lay_w] + [tok_t(w) for w in t_w]
                   + [tok_p(FOX_HEADS)] + [tok_s(w) for w in row_w]),
        out_shape=([sds((n, w), F32) for w in all_w] + [sds((n_p, w), F32) for w in lay_w]
                   + [sds((depth, bsz, w, seq), F32) for w in t_w] + [sds((depth, n_p, FOX_HEADS), F32)]
                   + [sds((depth, s_tiles * tm, w), F32) for w in row_w]),
        input_output_aliases=aliases,
        compiler_params=_params("arbitrary"),
        name="inproj",
    )(*args)
    return outs[:6], outs[6:n_plain], outs[n_plain:]


def _block_cumsum(x, n_blocks):
    rows = n_blocks * SUBLANES
    padded = -(-rows // LANES) * LANES
    if padded != rows:
        x = jnp.concatenate([x, jnp.zeros((padded - rows, LANES), F32)], axis=0)
    hi = functools.partial(jnp.dot, preferred_element_type=F32, precision=lax.Precision.HIGHEST)
    r0 = lax.broadcasted_iota(I32, (LANES, LANES), 0)
    c0 = lax.broadcasted_iota(I32, (LANES, LANES), 1)
    within = hi(x, jnp.where(r0 <= c0, 1.0, 0.0).astype(F32))
    totals = jnp.broadcast_to(within[:, LANES - 1:LANES], (padded, LANES))
    dist = lax.broadcasted_iota(I32, (padded, padded), 0) - lax.broadcasted_iota(I32, (padded, padded), 1)
    earlier = jnp.where((dist > 0) & ((dist & (SUBLANES - 1)) == 0), 1.0, 0.0).astype(F32)
    full = within + hi(earlier, totals)
    return [full[b * SUBLANES:(b + 1) * SUBLANES] for b in range(n_blocks)]


def _cumsum_kernel(x_ref, o_ref, *, n_blocks):
    o_ref[...] = jnp.concatenate(_block_cumsum(x_ref[...], n_blocks), axis=1)


def _cumsum_prompt(lf_blocks):
    b, rows, _ = lf_blocks.shape
    n_blocks = rows // FOX_HEADS
    t = n_blocks * LANES
    return pl.pallas_call(
        functools.partial(_cumsum_kernel, n_blocks=n_blocks),
        grid=(b,),
        in_specs=[pl.BlockSpec((None, rows, LANES), lambda i: (i, 0, 0))],
        out_specs=pl.BlockSpec((None, FOX_HEADS, t), lambda i: (i, 0, 0)),
        out_shape=jax.ShapeDtypeStruct((b, FOX_HEADS, t), F32),
        compiler_params=_params("parallel"),
        name="cumsum_prompt",
    )(lf_blocks)


def _q_block_diag(q):
    t = q.shape[0]
    lane = lax.broadcasted_iota(I32, (t, LANES), 1)
    zeros = jnp.zeros((t, LANES), F32)
    blocks = []
    for h in range(DSA_HEADS):
        g, r = divmod(h, 2)
        src = q[:, g * LANES:(g + 1) * LANES]
        if r != g % 2:
            src = pltpu.roll(src, HEAD_DIM, axis=1)
        keep = (lane < HEAD_DIM) if g % 2 == 0 else (lane >= HEAD_DIM)
        m = jnp.where(keep, src, 0.0)
        blocks.append(jnp.concatenate([m, zeros] if g // 2 == 0 else [zeros, m], axis=1))
    return jnp.concatenate(blocks, axis=0).astype(BF16)


def _head_rows(q):
    return jnp.concatenate([q[:, h * IDX_DIM:(h + 1) * IDX_DIM] for h in range(IDX_HEADS)], axis=0)


def _extract_heads(acc, t):
    outs = []
    for h in range(DSA_HEADS):
        g = h // 2
        outs.append(acc[h * t:(h + 1) * t, g * HEAD_DIM:(g + 1) * HEAD_DIM])
    return jnp.concatenate(outs, axis=1)


def _sort_key(score):
    bits = pltpu.bitcast(score, I32)
    return jnp.where(bits < 0, bits ^ 0x7FFFFFFF, bits)


def _indexer_score(s, wi, t):
    score = jnp.zeros((t, s.shape[1]), F32)
    for h in range(s.shape[0] // t):
        score = score + jnp.maximum(s[h * t:(h + 1) * t], 0.0) * wi[:, h:h + 1]
    return jnp.where(score == 0.0, 0.0, score)


def _fold_lanes(x):
    out = x[:, :LANES]
    for j in range(1, x.shape[1] // LANES):
        out = out + x[:, j * LANES:(j + 1) * LANES]
    return out


LOG2E = 1.4426950408889634


def _two_pass_attention(qbd, kt_ref, v_ref, loops, bias_fn, s_ref, mx_ref, l_ref, acc_ref, tq, tk):
    group = DSA_HEADS // DSA_KV_HEADS
    n_tiles = tk // LANES

    def fold(x, op):
        out = x[:, :, :LANES]
        for j in range(1, n_tiles):
            out = op(out, x[:, :, j * LANES:(j + 1) * LANES])
        return out

    def logit_chunk(c, carry, *, tag):
        kt = kt_ref[:, pl.ds(pl.multiple_of(c * tk, tk), tk)].astype(BF16)
        bias = bias_fn(c, tag)
        for g in range(DSA_HEADS // group):
            h0, h1 = g * group, (g + 1) * group
            s3 = _dot(qbd[h0 * tq:h1 * tq], kt).reshape(group, tq, tk) + bias(h0, h1)
            s_ref[c, h0:h1] = s3
            mx_ref[h0:h1] = jnp.maximum(mx_ref[h0:h1], fold(s3, jnp.maximum))
        return carry

    def pv_chunk(c, carry):
        v = v_ref[pl.ds(pl.multiple_of(c * tk, tk), tk), :].astype(BF16)
        for g in range(DSA_HEADS // group):
            h0, h1 = g * group, (g + 1) * group
            m = jnp.concatenate([mx_ref[h0:h1]] * n_tiles, axis=2)
            p = jnp.exp2(s_ref[c, h0:h1] - m)
            l_ref[h0:h1] += fold(p, jnp.add)
            acc_ref[h0:h1] += _dot(p.reshape(group * tq, tk).astype(BF16), v).reshape(group, tq, KV_WIDTH)
        return carry

    mx_ref[...] = jnp.full_like(mx_ref, NEG)
    for lo, hi, tag in loops:
        lax.fori_loop(lo, hi, functools.partial(logit_chunk, tag=tag), 0)
    mx_ref[...] = jnp.broadcast_to(jnp.max(mx_ref[...], axis=2, keepdims=True), mx_ref.shape)
    l_ref[...] = jnp.zeros_like(l_ref)
    acc_ref[...] = jnp.zeros_like(acc_ref)
    lax.fori_loop(loops[0][0], loops[-1][1], pv_chunk, 0)
    out = acc_ref[...] / jnp.sum(l_ref[...], axis=2, keepdims=True)
    return _extract_heads(out.reshape(DSA_HEADS * tq, KV_WIDTH), tq)


def _two_pass_scratch(tq, tk, nk):
    return [pltpu.VMEM((nk, DSA_HEADS, tq, tk), F32), pltpu.VMEM((DSA_HEADS, tq, LANES), F32),
            pltpu.VMEM((DSA_HEADS, tq, LANES), F32), pltpu.VMEM((DSA_HEADS, tq, KV_WIDTH), F32)]


def _fox_prompt_kernel(q_ref, kt_ref, v_ref, fq_ref, fk_ref, o_ref, s_ref, m_ref, l_ref, acc_ref, *, tq, tk):
    i = pl.program_id(1)
    qbd = _q_block_diag(q_ref[...] * (HEAD_DIM ** -0.5 * LOG2E))
    fq = fq_ref[...] * LOG2E
    fq_rep = jnp.stack([jnp.broadcast_to(fq[:, h:h + 1], (tq, LANES)) for h in range(FOX_HEADS)], axis=0)
    qpos = i * tq + lax.broadcasted_iota(I32, (tq, tk), 0)
    kiota = lax.broadcasted_iota(I32, (tq, tk), 1)

    def bias(c, masked):
        fk = fk_ref[c] * LOG2E
        fk3 = jnp.stack([fk[h:h + 1, :] for h in range(FOX_HEADS)], axis=0)
        causal = ((kiota + c * tk) <= qpos)[None]

        def heads(h0, h1):
            b = jnp.concatenate([fq_rep[h0:h1]] * (tk // LANES), axis=2) - fk3[h0:h1]
            return jnp.where(causal, b, NEG) if masked else b

        return heads

    n_full = (i * tq + 1) // tk
    loops = ((0, n_full, False), (n_full, (i * tq + tq - 1) // tk + 1, True))
    o_ref[...] = _two_pass_attention(qbd, kt_ref, v_ref, loops, bias, s_ref, m_ref, l_ref, acc_ref, tq, tk)


def _fox_prompt(q, kt, v, fq, fk, bsz, seq, l, *, tq, tk):
    n = q.shape[0]
    nq, nk = seq // tq, seq // tk
    return pl.pallas_call(
        functools.partial(_fox_prompt_kernel, tq=tq, tk=tk),
        grid=(bsz, nq),
        in_specs=[
            pl.BlockSpec((tq, Q_WIDTH), lambda bi, i: (bi * nq + i, 0)),
            pl.BlockSpec((None, None, KV_WIDTH, seq), lambda bi, i: (l, bi, 0, 0)),
            pl.BlockSpec((seq, KV_WIDTH), lambda bi, i: (bi, 0)),
            pl.BlockSpec((None, tq, FOX_HEADS), lambda bi, i: (bi, i, 0)),
            pl.BlockSpec((None, nk, FOX_HEADS, tk), lambda bi, i: (bi, 0, 0, 0)),
        ],
        out_specs=pl.BlockSpec((tq, Q_WIDTH), lambda bi, i: (bi * nq + i, 0)),
        out_shape=jax.ShapeDtypeStruct((n, Q_WIDTH), F32),
        scratch_shapes=_two_pass_scratch(tq, tk, nk),
        compiler_params=_params("parallel", "parallel"),
        name="fox_prompt",
    )(q, kt, v, fq, fk)


def _dsa_prompt_kernel(q_ref, qit_ref, wit_ref, kt_ref, v_ref, ki_ref, o_ref,
                       key_ref, thr_ref, cnt_ref, j_ref, s_ref, m_ref, l_ref, acc_ref, *, tq, tk, top_k, seq):
    i = pl.program_id(1)
    n_vis = (i * tq + tq - 1) // tk + 1
    kpos = lax.broadcasted_iota(I32, (tk, tq), 0)
    qpos = i * tq + lax.broadcasted_iota(I32, (tk, tq), 1)
    kf = float(top_k)

    qit = qit_ref[...].astype(BF16)
    qi_t = jnp.concatenate([qit[h * IDX_DIM:(h + 1) * IDX_DIM] for h in range(IDX_HEADS)], axis=1)
    wit = wit_ref[...] * (IDX_DIM ** -0.5)

    def score_chunk(c, carry):
        start = pl.multiple_of(c * tk, tk)
        s = _dot(ki_ref[pl.ds(start, tk), :].astype(BF16), qi_t)
        score = jnp.zeros((tk, tq), F32)
        for h in range(IDX_HEADS):
            score = score + jnp.maximum(s[:, h * tq:(h + 1) * tq], 0.0) * wit[h:h + 1, :]
        score = jnp.where(score == 0.0, 0.0, score)
        score = jnp.where((kpos + c * tk) <= qpos, score, -jnp.inf)
        key_ref[c] = _sort_key(score)
        return carry

    lax.fori_loop(0, n_vis, score_chunk, 0)

    def partial_count(mask):
        ones = jnp.where(mask, 1.0, 0.0).reshape(8, tk // (8 * SUBLANES), SUBLANES, tq)
        c = [jnp.sum(ones[g], axis=0) for g in range(8)]
        return ((c[0] + c[1]) + (c[2] + c[3])) + ((c[4] + c[5]) + (c[6] + c[7]))

    def count(*preds):
        def body(c, cnts):
            key, pos = key_ref[c], kpos + c * tk
            return tuple(cnt + partial_count(pred(key, pos)) for cnt, pred in zip(cnts, preds))
        zero = jnp.zeros((SUBLANES, tq), F32)
        totals = lax.fori_loop(0, n_vis, body, tuple(zero for _ in preds))
        return tuple(jnp.sum(x, axis=0, keepdims=True) for x in totals)

    def search(n_chunks):
        def step(b, carry):
            cand, cnt_at = carry
            trial = cand ^ lax.shift_left(jnp.int32(1), 31 - b)
            cnt = jnp.zeros((SUBLANES, tq), F32)
            for c in range(n_chunks):
                cnt = cnt + partial_count(key_ref[c] >= trial)
            total = jnp.sum(cnt, axis=0, keepdims=True)
            ok = total >= kf
            return jnp.where(ok, trial, cand), jnp.where(ok, total, cnt_at)

        carry = (jnp.full((1, tq), INT_MIN, I32), jnp.full((1, tq), float(n_chunks * tk), F32))
        for lo, hi in zip(SEARCH_STAGES[:-1], SEARCH_STAGES[1:]):
            if lo == 0:
                carry = lax.fori_loop(lo, hi, step, carry)
                thr_ref[...], cnt_ref[...] = carry
            else:
                @pl.when(jnp.max(jnp.where(cnt_ref[...] != kf, 1.0, 0.0)) > 0.0)
                def _():
                    thr_ref[...], cnt_ref[...] = lax.fori_loop(lo, hi, step, (thr_ref[...], cnt_ref[...]))

    for n_chunks in range(1, seq // tk + 1):
        @pl.when(n_vis == n_chunks)
        def _():
            search(n_chunks)

    thr = thr_ref[...]

    n_gt, n_ge = count(lambda key, pos: key > thr, lambda key, pos: key >= thr)
    need = kf - n_gt
    j_ref[...] = jnp.full_like(j_ref, seq)
    tied = jnp.where((n_ge > kf) & (thr != NEG_INF_KEY), 1.0, 0.0)

    @pl.when(jnp.max(tied) > 0.0)
    def _():
        def idx_step(b, lo):
            trial = lo | lax.shift_left(jnp.int32(1), int(math.log2(seq)) - 1 - b)
            (c,) = count(lambda key, pos: jnp.where(key == thr, pos, seq) < trial)
            return jnp.where(c < need, trial, lo)

        j_ref[...] = lax.fori_loop(0, int(math.log2(seq)), idx_step, jnp.zeros((1, tq), I32))

    jmax = j_ref[...]

    qbd = _q_block_diag(q_ref[...] * (HEAD_DIM ** -0.5 * LOG2E))

    def bias(c, tag):
        key = key_ref[c]
        tie_bias = jnp.where(jnp.where(key == thr, kpos + c * tk, seq + 1) <= jmax, 0.0, NEG)
        b = jnp.where(key > NEG_INF_KEY, jnp.where(key > thr, 0.0, tie_bias), NEG).T[None]
        return lambda h0, h1: b

    o_ref[...] = _two_pass_attention(qbd, kt_ref, v_ref, ((0, n_vis, None),), bias, s_ref, m_ref, l_ref, acc_ref,
                                     tq, tk)


def _dsa_prompt(q, qit, wit, kt, v, ki, bsz, seq, l, *, tq, tk):
    n = q.shape[0]
    nq, nk = seq // tq, seq // tk
    top_k = min(TOPK_MAX, seq // 4)
    return pl.pallas_call(
        functools.partial(_dsa_prompt_kernel, tq=tq, tk=tk, top_k=top_k, seq=seq),
        grid=(bsz, nq),
        in_specs=[
            pl.BlockSpec((tq, Q_WIDTH), lambda bi, i: (bi * nq + i, 0)),
            pl.BlockSpec((Q_WIDTH, tq), lambda bi, i: (0, bi * nq + i)),
            pl.BlockSpec((IDX_HEADS, tq), lambda bi, i: (0, bi * nq + i)),
            pl.BlockSpec((None, None, KV_WIDTH, seq), lambda bi, i: (l, bi, 0, 0)),
            pl.BlockSpec((seq, KV_WIDTH), lambda bi, i: (bi, 0)),
            pl.BlockSpec((seq, IDX_DIM), lambda bi, i: (bi, 0)),
        ],
        out_specs=pl.BlockSpec((tq, Q_WIDTH), lambda bi, i: (bi * nq + i, 0)),
        out_shape=jax.ShapeDtypeStruct((n, Q_WIDTH), F32),
        scratch_shapes=[pltpu.VMEM((nk, tk, tq), I32), pltpu.VMEM((1, tq), I32), pltpu.VMEM((1, tq), F32),
                        pltpu.VMEM((1, tq), I32)]
        + _two_pass_scratch(tq, tk, nk),
        compiler_params=_params("parallel", "parallel"),
        name="dsa_prompt",
    )(q, qit, wit, kt, v, ki)


def _page_specs(n_pages, rows, l):
    return [pl.BlockSpec((None, None, rows, PAGE_SIZE),
                         functools.partial(lambda b, pt, p: (l, pt[b * n_pages + p], 0, 0), p=p))
            for p in range(n_pages)]


def _pad_rows(x, rows):
    return jnp.concatenate([x, jnp.zeros((rows - x.shape[0], x.shape[1]), x.dtype)], axis=0)


def _paged_logits(lhs, pages_t, new_rows):
    blocks = [_dot(lhs, p[...].astype(BF16)) for p in pages_t]
    blocks.append(_nt_dot(lhs, _pad_rows(new_rows, PAGE_SIZE).astype(BF16)))
    return jnp.concatenate(blocks, axis=1)


def _softmax_pv(logits, v_pages_t, v_new, t):
    m = jnp.max(logits, axis=1, keepdims=True)
    p = jnp.exp(logits - m)
    l = jnp.sum(p, axis=1, keepdims=True)
    pb = p.astype(BF16)
    acc = jnp.zeros((logits.shape[0], KV_WIDTH), F32)
    for n, vp in enumerate(v_pages_t):
        acc = acc + _nt_dot(pb[:, n * PAGE_SIZE:(n + 1) * PAGE_SIZE], vp[...].astype(BF16))
    n = len(v_pages_t)
    acc = acc + _dot(pb[:, n * PAGE_SIZE:(n + 1) * PAGE_SIZE], _pad_rows(v_new, PAGE_SIZE).astype(BF16))
    return _extract_heads(acc / l, t)


def _visible(t, length, past):
    pos = lax.broadcasted_iota(I32, (t, length), 1)
    return pos, pos <= past + lax.broadcasted_iota(I32, (t, length), 0)


def _fox_sample_body(kp, vp, lp, q_ref, kn_ref, vn_ref, lfn_ref, t):
    n_pages = len(kp)
    past = n_pages * PAGE_SIZE
    length = past + PAGE_SIZE

    f_tiles = _block_cumsum(jnp.concatenate([p[...] for p in lp] + [lfn_ref[...]], axis=0), n_pages + 1)
    fk = jnp.concatenate(f_tiles, axis=1)
    lane = lax.broadcasted_iota(I32, (t, LANES), 1)
    diag = lane == lax.broadcasted_iota(I32, (t, LANES), 0)

    qbd = _q_block_diag(q_ref[...] * (HEAD_DIM ** -0.5))
    s = _paged_logits(qbd, kp, kn_ref[...])
    _, visible = _visible(t, length, past)
    rows = []
    for h in range(FOX_HEADS):
        f_new = jnp.broadcast_to(f_tiles[n_pages][h:h + 1, :], (t, LANES))
        fq = jnp.sum(jnp.where(diag, f_new, 0.0), axis=1, keepdims=True)
        sh = s[h * t:(h + 1) * t] + (fq - fk[h:h + 1, :])
        rows.append(jnp.where(visible, sh, NEG))
    return _softmax_pv(jnp.concatenate(rows, axis=0), vp, vn_ref[...], t)


def _sample_topk(key, pos, top_k, length, j_ref):
    t = key.shape[0]
    kf = float(top_k)
    count = lambda mask: jnp.sum(jnp.where(mask, 1.0, 0.0), axis=1, keepdims=True)

    def digit_pass(cand, shift, n_trials):
        digit = jnp.zeros((t, 1), I32)
        for j in range(1, n_trials + 1):
            trial = cand + jnp.left_shift(jnp.int32(j), shift)
            digit = digit + jnp.where(count(key >= trial) >= kf, 1, 0)
        return cand + jnp.left_shift(digit, shift)

    thr = digit_pass(jnp.full((t, 1), INT_MIN, I32), 30, 3)
    for shift in range(27, -1, -3):
        thr = digit_pass(thr, shift, 7)

    need = kf - count(key > thr)
    n_ge = count(key >= thr)
    j_ref[...] = jnp.full_like(j_ref, length)
    tied = jnp.where((n_ge > kf) & (thr != NEG_INF_KEY), 1.0, 0.0)

    @pl.when(jnp.max(tied) > 0.0)
    def _():
        n_bits = int(math.ceil(math.log2(length)))
        tie_pos = jnp.where(key == thr, pos, length)

        def idx_step(b, lo):
            trial = lo | lax.shift_left(jnp.int32(1), n_bits - 1 - b)
            return jnp.where(count(tie_pos < trial) < need, trial, lo)

        j_ref[...] = lax.fori_loop(0, n_bits, idx_step, jnp.zeros((t, 1), I32))

    return thr, j_ref[...]


def _dsa_sample_body(kp, vp, ip, q_ref, qi_ref, wi_ref, kn_ref, vn_ref, in_ref, j_ref, t, top_k):
    n_pages = len(kp)
    past = n_pages * PAGE_SIZE
    length = past + PAGE_SIZE

    qi = _head_rows(qi_ref[...]).astype(BF16)
    score = _indexer_score(_paged_logits(qi, ip, in_ref[...]), wi_ref[...] * (IDX_DIM ** -0.5), t)
    pos, visible = _visible(t, length, past)
    key = _sort_key(jnp.where(visible, score, -jnp.inf))
    thr, jmax = _sample_topk(key, pos, top_k, length, j_ref)
    tie_bias = jnp.where(jnp.where(key == thr, pos, length + 1) <= jmax, 0.0, NEG)
    bias = jnp.where(key > NEG_INF_KEY, jnp.where(key > thr, 0.0, tie_bias), NEG)

    qbd = _q_block_diag(q_ref[...] * (HEAD_DIM ** -0.5))
    logits = _paged_logits(qbd, kp, kn_ref[...]) + jnp.concatenate([bias] * DSA_HEADS, axis=0)
    return _softmax_pv(logits, vp, vn_ref[...], t)


def _sample_attn_kernel(pt_ref, *refs, n_pages, t, top_k):
    pages = [refs[n * n_pages:(n + 1) * n_pages] for n in range(6)]
    kd, vd, ki, kf, vf, lf = pages
    (qd_ref, qi_ref, wi_ref, kdn_ref, vdn_ref, kin_ref, qf_ref, kfn_ref, vfn_ref, lfn_ref, _, _,
     od_ref, of_ref, j_ref) = refs[6 * n_pages:]
    of_ref[...] = _fox_sample_body(kf, vf, lf, qf_ref, kfn_ref, vfn_ref, lfn_ref, t)
    od_ref[...] = _dsa_sample_body(kd, vd, ki, qd_ref, qi_ref, wi_ref, kdn_ref, vdn_ref, kin_ref, j_ref, t, top_k)


def _sample_attn(page_table, caches_t, qd, qi, wi, kd, vd, ki, qf, kf, vf, lf_new_t, o_dsa, o_fox, n_p, l):
    r, n_pages = page_table.shape
    t = (qd.shape[0] - n_p) // r
    off = n_p // t
    top_k = min(TOPK_MAX, (n_pages * PAGE_SIZE + t) // 4)
    req = lambda w: pl.BlockSpec((t, w), lambda bi, pt: (off + bi, 0))
    page_in = []
    for c in caches_t:
        page_in += _page_specs(n_pages, c.shape[2], l)
    new = lambda w: pl.BlockSpec((None, t, w), lambda bi, pt: (l, bi, 0))
    row_specs = [req(Q_WIDTH), req(Q_WIDTH), req(IDX_HEADS), new(KV_WIDTH), new(KV_WIDTH), new(IDX_DIM),
                 req(Q_WIDTH), new(KV_WIDTH), new(KV_WIDTH)]
    grid_spec = pltpu.PrefetchScalarGridSpec(
        num_scalar_prefetch=1,
        grid=(r,),
        in_specs=(page_in + row_specs
                  + [pl.BlockSpec((None, FOX_HEADS, LANES), lambda bi, pt: (bi, 0, 0)),
                     pl.BlockSpec(memory_space=pl.ANY), pl.BlockSpec(memory_space=pl.ANY)]),
        out_specs=[req(Q_WIDTH), req(Q_WIDTH)],
        scratch_shapes=[pltpu.VMEM((t, 1), I32)],
    )
    n_in = 1 + 6 * n_pages + len(row_specs) + 3
    page_args = [c for c in caches_t for _ in range(n_pages)]
    return pl.pallas_call(
        functools.partial(_sample_attn_kernel, n_pages=n_pages, t=t, top_k=top_k),
        grid_spec=grid_spec,
        out_shape=[jax.ShapeDtypeStruct(o_dsa.shape, F32), jax.ShapeDtypeStruct(o_fox.shape, F32)],
        input_output_aliases={n_in - 2: 0, n_in - 1: 1},
        compiler_params=_params("parallel"),
        name="sample_attn",
    )(page_table.reshape(-1), *page_args, qd, qi, wi, kd, vd, ki, qf, kf, vf, lf_new_t, o_dsa, o_fox)


def _merge_kernel(h_ref, od_ref, of_ref, zb_ref, z_ref, zprev_ref, e1_ref, e2_ref, gpre_ref, gpost_ref, cw_ref,
                  wg_ref, wbd_ref, wbc_ref, wbf_ref, wo_ref, o_ref, *, prompt_tiles, seq_tiles, ts):
    tm, d = h_ref.shape

    i = pl.program_id(0)
    is_sample = i >= prompt_tiles
    z = z_ref[...]
    row = lax.broadcasted_iota(I32, z.shape, 0)
    t = row & (jnp.where(is_sample, ts, tm) - 1)
    keep = jnp.where(jnp.logical_or(is_sample, i % seq_tiles == 0), 0.0, 1.0)
    prev = zprev_ref[...] * keep
    p1, p2 = prev[SUBLANES - 1:SUBLANES], prev[SUBLANES - 2:SUBLANES - 1]
    r1 = jnp.where(is_sample, e1_ref[...], jnp.broadcast_to(p1, z.shape))
    r2 = jnp.where(is_sample, e2_ref[...], jnp.where(t == 0, p2, p1))
    z1 = jnp.where(t == 0, r1, pltpu.roll(z, 1, axis=0))
    z2 = jnp.where(t < 2, r2, pltpu.roll(z, 2, axis=0))
    cw = cw_ref[...]
    conv = cw[0:1] * z2 + cw[1:2] * z1 + cw[2:3] * z

    o_conv = zb_ref[...] * conv

    rows = tm // MERGE_ROW_CHAINS
    for c in range(MERGE_ROW_CHAINS):
        sl = slice(c * rows, (c + 1) * rows)
        h = h_ref[sl]
        xn = _rms(h, gpre_ref[...]).astype(BF16)
        branches = ((od_ref[sl], wbd_ref), (o_conv[sl], wbc_ref), (of_ref[sl], wbf_ref))
        merged = jnp.zeros_like(h)
        for n, (val, w_ref) in enumerate(branches):
            gate = jax.nn.sigmoid(_dot(xn, wg_ref[:, n * d:(n + 1) * d]))
            merged = merged + gate * _dot(val.astype(BF16), w_ref[...])
        o_ref[sl] = h + _rms(_dot(merged.astype(BF16), wo_ref[...]), gpost_ref[...])


def _merge(h, od, of, zb, z, e1, e2, g_pre, g_post, conv_w, wg, wbd, wbc, wbf, wo, l, n_p, seq, ts, *, tm):
    n, d = h.shape
    assert seq % tm == 0 and tm % ts == 0 and tm & (tm - 1) == 0 and ts & (ts - 1) == 0 and CONV_K == 3
    p_tiles = n_p // tm
    tok = lambda w: pl.BlockSpec((tm, w), lambda i: (i, 0))
    lay = lambda a, b: pl.BlockSpec((None, a, b), lambda i: (l, 0, 0))
    halo = pl.BlockSpec((SUBLANES, CONV_WIDTH), lambda i: (jnp.maximum(i * (tm // SUBLANES) - 1, 0), 0))
    smp = pl.BlockSpec((tm, CONV_WIDTH), lambda i: (jnp.maximum(i - p_tiles, 0), 0))
    return pl.pallas_call(
        functools.partial(_merge_kernel, prompt_tiles=p_tiles, seq_tiles=seq // tm, ts=ts),
        grid=(n // tm,),
        in_specs=[tok(d), tok(Q_WIDTH), tok(Q_WIDTH), tok(CONV_WIDTH), tok(CONV_WIDTH), halo, smp, smp,
                  lay(1, d), lay(1, d), lay(CONV_K, CONV_WIDTH),
                  lay(d, N_BRANCHES * d), lay(Q_WIDTH, d), lay(CONV_WIDTH, d), lay(Q_WIDTH, d), lay(d, d)],
        out_specs=tok(d),
        out_shape=jax.ShapeDtypeStruct((n, d), F32),
        compiler_params=_params("parallel"),
        name="merge",
    )(h, od, of, zb, z, z, e1, e2, g_pre, g_post, conv_w, wg, wbd, wbc, wbf, wo)


def _rope_tables(pos):
    half = ROPE_DIM // 2
    inv = jnp.power(jnp.float32(ROPE_THETA), -jnp.arange(half, dtype=jnp.float32) * (2.0 / ROPE_DIM))
    ang = pos.astype(jnp.float32)[:, None] * inv[None, :]
    cos, sin = jnp.cos(ang), jnp.sin(ang)
    n = pos.shape[0]
    ones = jnp.ones((n, HEAD_DIM - ROPE_DIM), F32)
    zeros_r = jnp.zeros((n, HEAD_DIM - ROPE_DIM), F32)
    zeros_h = jnp.zeros((n, half), F32)
    cos_t = jnp.concatenate([cos, cos, ones], axis=1)
    sa_t = jnp.concatenate([-sin, zeros_h, zeros_r], axis=1)
    sb_t = jnp.concatenate([zeros_h, sin, zeros_r], axis=1)
    rep = LANES // HEAD_DIM
    return tuple(jnp.tile(x, (1, rep)) for x in (cos_t, sa_t, sb_t))


def _pick(n, prefs):
    for p in prefs:
        if n % p == 0:
            return p
    return n


def kernel(x_prompt, x_sample, cache_dsa_k, cache_dsa_v, cache_idx_k, cache_fox_k, cache_fox_v, cache_fox_logf,
           state_conv, page_table, g_ffn1_pre, g_ffn1_post, w_ffn1_gu, w_ffn1_dn, g_mix_pre, g_mix_post, w_in,
           b_forget, conv_w, w_br_dsa, w_br_conv, w_br_fox, w_out, g_ffn2_pre, g_ffn2_post, w_ffn2_gu, w_ffn2_dn):
    bsz, seq, d = x_prompt.shape
    r, ts, _ = x_sample.shape
    depth = w_in.shape[0]
    n_pool = cache_dsa_k.shape[1]
    n_pages = page_table.shape[1]
    past = n_pages * PAGE_SIZE
    n_p, n_s = bsz * seq, r * ts
    dff = w_ffn1_dn.shape[1]
    assert ts == SUBLANES and seq % LANES == 0

    tm = _pick(math.gcd(n_p, n_s), (512, 256, 128, 64, 32, 16, 8))
    tf = _pick(dff, (1408, 1024, 512, 256, 128))
    tq = _pick(seq, (128,))
    tk = _pick(seq, (512, 256, 128))
    nk = seq // tk

    sizes = (Q_WIDTH, KV_WIDTH, KV_WIDTH, IDX_HEADS * IDX_DIM, IDX_DIM, IDX_HEADS, CONV_WIDTH, CONV_WIDTH,
             CONV_WIDTH, Q_WIDTH, KV_WIDTH, KV_WIDTH, FOX_HEADS, N_BRANCHES * d)
    offs = [0]
    for s_ in sizes:
        offs.append(offs[-1] + s_)
    col = lambda i: w_in[:, :, offs[i]:offs[i + 1]]
    (c_qd, c_kd, c_vd, c_qi, c_ki, c_wi, c_zb, c_zc, c_zx, c_qf, c_kf, c_vf, c_fl, c_g) = [col(i) for i in range(14)]
    ki_pad = jnp.concatenate([c_ki, jnp.zeros((depth, d, LANES - IDX_DIM), F32)], axis=2)
    w_rope = jnp.concatenate([c_qd, c_kd, c_qi, ki_pad], axis=2).astype(BF16)
    w_plain = jnp.concatenate([c_vd, c_zb, c_qf, c_kf, c_vf], axis=2).astype(BF16)
    w_z = jnp.concatenate([c_zc, c_zx], axis=2).astype(BF16)
    w_wi, w_fl, w_g = c_wi.astype(BF16), c_fl.astype(BF16), c_g.astype(BF16)
    w1gu, w1dn = w_ffn1_gu.astype(BF16), w_ffn1_dn.astype(BF16)
    w2gu, w2dn = w_ffn2_gu.astype(BF16), w_ffn2_dn.astype(BF16)
    wbd, wbc, wbf, wo = (w.astype(BF16) for w in (w_br_dsa, w_br_conv, w_br_fox, w_out))
    row = lambda g: g.reshape(depth, 1, -1)
    g1pre, g1post, gmpre, gmpost, g2pre, g2post, bfg = (
        row(g) for g in (g_ffn1_pre, g_ffn1_post, g_mix_pre, g_mix_post, g_ffn2_pre, g_ffn2_post, b_forget))

    pos = jnp.concatenate([jnp.tile(jnp.arange(seq, dtype=I32), bsz),
                           jnp.tile(past + jnp.arange(ts, dtype=I32), r)])
    cos_t, sa_t, sb_t = _rope_tables(pos)

    page_t = lambda c: jnp.moveaxis(c, 2, -1).reshape(depth, n_pool, -1, PAGE_SIZE)
    caches_t = [page_t(c) for c in (cache_dsa_k, cache_dsa_v, cache_idx_k, cache_fox_k, cache_fox_v, cache_fox_logf)]

    x = jnp.concatenate([x_prompt.reshape(n_p, d), x_sample.reshape(n_s, d)], axis=0)
    stacks, tails_p, tails_s = None, [], []
    for l in range(depth):
        h = _ffn(x, g1pre, g1post, w1gu, w1dn, l, tm=tm, tf=tf)
        (qd, qi, zb, qf, z, wi), (vd_l, ki_l, vf_l), stacks = _inproj(
            h, gmpre, cos_t, sa_t, sb_t, bfg, w_rope, w_plain, w_z, w_wi, w_fl, l, bsz, seq, stacks, tm=tm)
        kd_t, vd_t, ki_t, kf_t, vf_t, lf_p, kd_s, vd_s, ki_s, kf_s, vf_s, lf_s = stacks

        o_dsa = _dsa_prompt(qd, qi.T, wi.T, kd_t, vd_l, ki_l, bsz, seq, l, tq=tq, tk=tk)
        lf_blocks = lf_p[l].reshape(bsz, seq // LANES, LANES, FOX_HEADS).transpose(0, 1, 3, 2)
        f_k = _cumsum_prompt(lf_blocks.reshape(bsz, seq // LANES * FOX_HEADS, LANES))
        f_q = jnp.swapaxes(f_k, 1, 2)
        f_kc = f_k.reshape(bsz, FOX_HEADS, nk, tk).transpose(0, 2, 1, 3)
        o_fox = _fox_prompt(qf, kf_t, vf_l, f_q, f_kc, bsz, seq, l, tq=tq, tk=tk)
        lf_new_t = jnp.pad(jnp.swapaxes(lf_s[l].reshape(r, ts, FOX_HEADS), 1, 2), ((0, 0), (0, 0), (0, LANES - ts)))
        o_dsa, o_fox = _sample_attn(page_table, caches_t, qd, qi, wi, kd_s, vd_s, ki_s, qf, kf_s, vf_s, lf_new_t,
                                    o_dsa, o_fox, n_p, l)

        st = state_conv[l].astype(F32)
        pad_t = lambda a: jnp.pad(a, ((0, 0), (0, ts - a.shape[1]), (0, 0))).reshape(n_s, CONV_WIDTH)
        h = _merge(h, o_dsa, o_fox, zb, z, pad_t(st[:, 1:]), pad_t(st), gmpre, gmpost, conv_w,
                   w_g, wbd, wbc, wbf, wo, l, n_p, seq, ts, tm=tm)
        x = _ffn(h, g2pre, g2post, w2gu, w2dn, l, tm=tm, tf=tf)

        tails_p.append(z[:n_p].reshape(bsz, seq, CONV_WIDTH)[:, seq - (CONV_K - 1):])
        tails_s.append(z[n_p:].reshape(r, ts, CONV_WIDTH)[:, ts - (CONV_K - 1):])

    heads_t = lambda a: a.reshape(depth, bsz, DSA_KV_HEADS, HEAD_DIM, seq).transpose(0, 1, 4, 2, 3)
    kd_t, vd_t, ki_t, kf_t, vf_t, lf_p, kd_s, vd_s, ki_s, kf_s, vf_s, lf_s = stacks
    sp = [heads_t(kd_t), heads_t(vd_t), ki_t.transpose(0, 1, 3, 2), heads_t(kf_t), heads_t(vf_t),
          lf_p.reshape(depth, bsz, seq, FOX_HEADS), jnp.stack(tails_p, axis=0)]
    heads_s = lambda a: a.reshape(depth, r, ts, DSA_KV_HEADS, HEAD_DIM)
    ss = [heads_s(kd_s), heads_s(vd_s), ki_s.reshape(depth, r, ts, IDX_DIM), heads_s(kf_s), heads_s(vf_s),
          lf_s.reshape(depth, r, ts, FOX_HEADS), jnp.stack(tails_s, axis=0)]
    return (x[:n_p].reshape(bsz, seq, d), x[n_p:].reshape(r, ts, d),
            sp[0], sp[1], sp[2], sp[3], sp[4], sp[5], sp[6],
            ss[0], ss[1], ss[2], ss[3], ss[4], ss[5], ss[6])
```
